```python
import math
import jax, jax.numpy as jnp
from jax import lax
import numpy as np

D_MODEL = 1024
BATCH = 8
SEQ = 8192
DEPTH = 1

CHUNK = 64
N_META = 16
Q_BLOCK = 128
DA_HEADS = 4
DA_QK_DIM = 64
DA_V_DIM = 2 * DA_QK_DIM
DA_WIDTH = DA_HEADS * DA_V_DIM
GLA_HEADS = 4
GLA_K_DIM = 64
GLA_V_DIM = 128
GLA_WIDTH = GLA_HEADS * GLA_V_DIM
GLA_GATE_RANK = 16
GLA_GATE_TAU = 16.0
MIX_WIDTH = DA_WIDTH + GLA_WIDTH
IN_SIZES = (
    DA_HEADS * 2 * DA_QK_DIM,
    DA_HEADS * 2 * DA_QK_DIM,
    DA_HEADS * DA_V_DIM,
    GLA_HEADS * GLA_K_DIM,
    GLA_HEADS * GLA_K_DIM,
    GLA_HEADS * GLA_V_DIM,
    GLA_HEADS * GLA_V_DIM,
    GLA_GATE_RANK,
)
IN_COLS = sum(IN_SIZES)
D_FF = 2816
CONV_WIDTH = 3
RMS_EPS = 1e-6

kernel_name = "hymba_diffattn_gla_convglu_stream"


def _rmsnorm(x, w):
    xf = x.astype(jnp.float32)
    y = xf * lax.rsqrt(jnp.mean(xf * xf, axis=-1, keepdims=True) + RMS_EPS)
    return (y * w.astype(jnp.float32)).astype(x.dtype)


def _chunk_id(pos):
    return (pos - N_META) // CHUNK + 1


def _diff_attention(q, k, v, lam, subln_w, lam_init):
    f32 = jnp.float32
    B_, L, _ = q.shape
    Lp = -(-L // Q_BLOCK) * Q_BLOCK
    pad = Lp - L
    nb = Lp // Q_BLOCK

    def two_maps(t):
        t = jnp.pad(t.astype(f32), ((0, 0), (0, pad), (0, 0)))
        return t.reshape(B_, Lp, DA_HEADS, 2, DA_QK_DIM).transpose(3, 0, 2, 1, 4)

    qh = two_maps(q) * (DA_QK_DIM ** -0.5)
    kh = two_maps(k)
    k1, k2 = kh[0], kh[1]
    vh = jnp.pad(v.astype(f32), ((0, 0), (0, pad), (0, 0))).reshape(
        B_, Lp, DA_HEADS, DA_V_DIM).transpose(0, 2, 1, 3)

    pos = jnp.arange(Lp)
    kchunk = _chunk_id(pos)
    slopes = 2.0 ** (-8.0 * jnp.arange(1, DA_HEADS + 1, dtype=f32) / DA_HEADS)

    def blocks(t):
        return t.reshape(B_, DA_HEADS, nb, Q_BLOCK, -1).transpose(2, 0, 1, 3, 4)

    q1b, q2b = blocks(qh[0]), blocks(qh[1])
    starts = jnp.arange(nb) * Q_BLOCK

    def one_block(args):
        q1i, q2i, start = args
        qpos = start + jnp.arange(Q_BLOCK)
        visible = kchunk[None, :] <= _chunk_id(qpos)[:, None]
        dist = jnp.abs(qpos[:, None] - pos[None, :]).astype(f32)
        bias = -slopes[:, None, None] * dist

        def probs(qi, kk):
            s = jnp.einsum('bhqd,bhkd->bhqk', qi, kk) + bias
            s = jnp.where(visible, s, -jnp.inf)
            return jax.nn.softmax(s, axis=-1)

        p = probs(q1i, k1) - lam * probs(q2i, k2)
        return jnp.einsum('bhqk,bhkv->bhqv', p, vh)

    o = lax.map(one_block, (q1b, q2b, starts))
    o = o.transpose(1, 2, 0, 3, 4).reshape(B_, DA_HEADS, Lp, DA_V_DIM)[:, :, :L]
    o = _rmsnorm(o, subln_w) * (1.0 - lam_init)
    return o.transpose(0, 2, 1, 3).reshape(B_, L, DA_WIDTH).astype(q.dtype)


def _gla(q, k, v, r, g_lr, gate_w, gate_b, norm_w):
    f32 = jnp.float32
    B_, L, _ = q.shape
    log_a = jax.nn.log_sigmoid(g_lr.astype(f32) @ gate_w.astype(f32) + gate_b.astype(f32)) / GLA_GATE_TAU
    front = CHUNK - N_META
    back = (-(L + front)) % CHUNK
    Lg = L + front + back
    nc = Lg // CHUNK

    def heads(t, d):
        t = jnp.pad(t.astype(f32), ((0, 0), (front, back), (0, 0)))
        return t.reshape(B_, nc, CHUNK, GLA_HEADS, d).transpose(0, 3, 1, 2, 4)

    qh = heads(q, GLA_K_DIM) * (GLA_K_DIM ** -0.5)
    kh = heads(k, GLA_K_DIM)
    vh = heads(v, GLA_V_DIM)
    b = jnp.cumsum(heads(log_a, GLA_K_DIM), axis=3)
    q_dec = qh * jnp.exp(b)
    causal = jnp.tril(jnp.ones((CHUNK, CHUNK), dtype=bool))
    a = jnp.einsum('bhncd,bhnsd->bhncs', q_dec, kh * jnp.exp(-b))
    o_intra = jnp.einsum('bhncs,bhnsv->bhncv', jnp.where(causal, a, 0.0), vh)

    b_last = b[:, :, :, -1:, :]
    delta = jnp.einsum('bhncd,bhncv->bhndv', kh * jnp.exp(b_last - b), vh)
    chunk_decay = jnp.exp(b_last[:, :, :, 0, :])

    def step(state, inp):
        dec, dlt = inp
        return dec[..., None] * state + dlt, state

    s0 = jnp.zeros((B_, GLA_HEADS, GLA_K_DIM, GLA_V_DIM), f32)
    _, s_start = lax.scan(step, s0, (chunk_decay.transpose(2, 0, 1, 3),
                                     delta.transpose(2, 0, 1, 3, 4)))
    s_start = s_start.transpose(1, 2, 0, 3, 4)
    o_inter = jnp.einsum('bhncd,bhndv->bhncv', q_dec, s_start)

    o = (o_intra + o_inter).transpose(0, 2, 3, 1, 4).reshape(B_, Lg, GLA_HEADS, GLA_V_DIM)
    o = _rmsnorm(o[:, front:front + L], norm_w).reshape(B_, L, GLA_WIDTH)
    return (o * jax.nn.silu(r.astype(f32))).astype(q.dtype)


def _hybrid_mixer(h, norm_w, w_in, lq1, lk1, lq2, lk2, subln_w, gate_w, gate_b, gla_norm_w, w_out, layer):
    u = _rmsnorm(h, norm_w)
    proj = u @ w_in
    split_points = np.cumsum(IN_SIZES)[:-1].tolist()
    da_q, da_k, da_v, g_q, g_k, g_v, g_r, g_lr = jnp.split(proj, split_points, axis=-1)
    f32 = jnp.float32
    lam_init = 0.8 - 0.6 * math.exp(-0.3 * layer)
    lam = (jnp.exp(jnp.sum(lq1.astype(f32) * lk1.astype(f32)))
           - jnp.exp(jnp.sum(lq2.astype(f32) * lk2.astype(f32))) + lam_init)
    o_da = _diff_attention(da_q, da_k, da_v, lam, subln_w, lam_init)
    o_gla = _gla(g_q, g_k, g_v, g_r, g_lr, gate_w, gate_b, gla_norm_w)
    return jnp.concatenate([o_da, o_gla], axis=-1) @ w_out


def _conv_glu(h, norm_w, w_up, conv_w, conv_b, w_down):
    u = _rmsnorm(h, norm_w) @ w_up
    u = lax.conv_general_dilated(
        u, conv_w[:, None, :].astype(u.dtype), window_strides=(1,),
        padding=[(CONV_WIDTH - 1, 0)], dimension_numbers=('NWC', 'WIO', 'NWC'),
        feature_group_count=2 * D_FF) + conv_b
    val, gate = jnp.split(u, 2, axis=-1)
    return (val * jax.nn.silu(gate)) @ w_down


def setup_inputs(seed: int = 0) -> dict:
    key = jax.random.key(seed)
    ks = jax.random.split(key, 20)
    f32 = jnp.float32
    nrm = lambda k, shape, s: jax.random.normal(k, shape, f32) * s
    return {
        "x": nrm(ks[0], (BATCH, SEQ, D_MODEL), 1.0),
        "meta_tokens": nrm(ks[1], (N_META, D_MODEL), 1.0),
        "norm1_w": 1.0 + nrm(ks[2], (DEPTH, D_MODEL), 0.02),
        "w_in": nrm(ks[3], (DEPTH, D_MODEL, IN_COLS), D_MODEL ** -0.5),
        "lambda_q1": nrm(ks[4], (DEPTH, DA_QK_DIM), 0.1),
        "lambda_k1": nrm(ks[5], (DEPTH, DA_QK_DIM), 0.1),
        "lambda_q2": nrm(ks[6], (DEPTH, DA_QK_DIM), 0.1),
        "lambda_k2": nrm(ks[7], (DEPTH, DA_QK_DIM), 0.1),
        "da_subln_w": 1.0 + nrm(ks[8], (DEPTH, DA_V_DIM), 0.02),
        "gla_gate_w": nrm(ks[9], (DEPTH, GLA_GATE_RANK, GLA_HEADS * GLA_K_DIM), GLA_GATE_RANK ** -0.5),
        "gla_gate_b": nrm(ks[10], (DEPTH, GLA_HEADS * GLA_K_DIM), 0.01),
        "gla_norm_w": 1.0 + nrm(ks[11], (DEPTH, GLA_V_DIM), 0.02),
        "w_out": nrm(ks[12], (DEPTH, MIX_WIDTH, D_MODEL), MIX_WIDTH ** -0.5),
        "norm2_w": 1.0 + nrm(ks[13], (DEPTH, D_MODEL), 0.02),
        "w_up": nrm(ks[14], (DEPTH, D_MODEL, 2 * D_FF), D_MODEL ** -0.5),
        "conv_w": nrm(ks[15], (DEPTH, CONV_WIDTH, 2 * D_FF), CONV_WIDTH ** -0.5),
        "conv_b": nrm(ks[16], (DEPTH, 2 * D_FF), 0.01),
        "w_down": nrm(ks[17], (DEPTH, D_FF, D_MODEL), D_FF ** -0.5),
        "final_norm_w": 1.0 + nrm(ks[18], (D_MODEL,), 0.02),
    }


def reference(x, meta_tokens, norm1_w, w_in, lambda_q1, lambda_k1, lambda_q2, lambda_k2,
              da_subln_w, gla_gate_w, gla_gate_b, gla_norm_w, w_out, norm2_w, w_up,
              conv_w, conv_b, w_down, final_norm_w):
    b_ = x.shape[0]
    meta = jnp.broadcast_to(meta_tokens.astype(x.dtype)[None], (b_, N_META, D_MODEL))
    h = jnp.concatenate([meta, x], axis=1)
    for l in range(DEPTH):
        h = h + _hybrid_mixer(h, norm1_w[l], w_in[l], lambda_q1[l], lambda_k1[l],
                              lambda_q2[l], lambda_k2[l], da_subln_w[l], gla_gate_w[l],
                              gla_gate_b[l], gla_norm_w[l], w_out[l], l)
        h = h + _conv_glu(h, norm2_w[l], w_up[l], conv_w[l], conv_b[l], w_down[l])
    return _rmsnorm(h, final_norm_w)[:, N_META:]
```

```python
import functools
import math

import numpy as np
import jax
import jax.numpy as jnp
from jax import lax
from jax.experimental import pallas as pl
from jax.experimental.pallas import tpu as pltpu

F32 = jnp.float32
BF16 = jnp.bfloat16

D_MODEL = 1024
N_META = 16
CHUNK = 64
Q_BLOCK = 128
DA_HEADS = 4
DA_QK_DIM = 64
DA_V_DIM = 128
DA_WIDTH = DA_HEADS * DA_V_DIM
GLA_HEADS = 4
GLA_K_DIM = 64
GLA_V_DIM = 128
GLA_KW = GLA_HEADS * GLA_K_DIM
GLA_WIDTH = GLA_HEADS * GLA_V_DIM
GLA_GATE_RANK = 16
GLA_GATE_TAU = 16.0
D_FF = 2816
RMS_EPS = 1e-6
LAM_INIT = 0.8 - 0.6 * math.exp(-0.3 * 0)

LOG2E = 1.4426950408889634
Q_SCALE = DA_QK_DIM ** -0.5 * LOG2E
GLA_Q_SCALE = GLA_K_DIM ** -0.5
LANE = 128
KEY_TILE = 512
GLA_TILE = 512
TOK_TILE = 512
FF_CHUNK = 256
NAT_COLS = 2048 + LANE
VMEM_LIMIT = 56 * 1024 * 1024

_NT = (((1,), (1,)), ((), ()))
_TN = (((0,), (0,)), ((), ()))


def _dot(a, b):
    return jnp.dot(a, b, preferred_element_type=F32)


def _dot_nt(a, b):
    return lax.dot_general(a, b, _NT, preferred_element_type=F32)


def _dot_tn(a, b):
    return lax.dot_general(a, b, _TN, preferred_element_type=F32)


def _rms(x, w):
    return x * lax.rsqrt(jnp.mean(x * x, axis=-1, keepdims=True) + RMS_EPS) * w


def _log_sigmoid(z):
    return jnp.minimum(z, 0.0) - jnp.log(1.0 + jnp.exp(-jnp.abs(z)))


def _silu(g):
    return g * (1.0 / (1.0 + jnp.exp(-g)))


def _split3(x):
    hi = x.astype(BF16)
    r = x - hi.astype(F32)
    mid = r.astype(BF16)
    lo = (r - mid.astype(F32)).astype(BF16)
    return hi, mid, lo


def _cumsum_rows(tri, x):
    hi, mid, lo = _split3(x)
    return _dot(tri, hi) + _dot(tri, mid) + _dot(tri, lo)


def _lam(lam4):
    s1 = jnp.sum(lam4[0:1] * lam4[1:2], axis=-1, keepdims=True)
    s2 = jnp.sum(lam4[2:3] * lam4[3:4], axis=-1, keepdims=True)
    return jnp.exp(s1) - jnp.exp(s2) + LAM_INIT


def _head_mask(x, h, width):
    lane = lax.broadcasted_iota(jnp.int32, x.shape, 1)
    return jnp.where((lane >= h * width) & (lane < (h + 1) * width), x, jnp.zeros_like(x))


def _meta_kernel(meta_ref, n1_ref, wnat_ref, wt_ref, gw_ref, gb_ref, lam4_ref, c0_ref, subw_ref,
                 gnw_ref, smask_ref, wout_ref, n2_ref, wup_ref,
                 k_out, v_out, st_out, ut_out):
    m = N_META
    x = meta_ref[...]
    ub = _rms(x, n1_ref[...]).astype(BF16)
    nat = _dot(ub, wnat_ref[...])
    qv = _dot_nt(ub, wt_ref[...])
    kb = nat[:, 0:512].astype(BF16)
    qb = (qv[:, 0:512] * Q_SCALE).astype(BF16)
    vb = qv[:, 512:1024].astype(BF16)
    k_out[...] = kb
    v_out[...] = vb

    lam = _lam(lam4_ref[...])
    ii = lax.broadcasted_iota(jnp.int32, (m, m), 0)
    jj = lax.broadcasted_iota(jnp.int32, (m, m), 1)
    dist = jnp.abs(ii - jj).astype(F32)

    def softmax2(s):
        p = jnp.exp2(s - jnp.max(s, axis=-1, keepdims=True))
        return p / jnp.sum(p, axis=-1, keepdims=True)

    o_da = []
    for h in range(DA_HEADS):
        bias = -c0_ref[h] * dist
        p1 = softmax2(_dot_nt(_head_mask(qb, 2 * h, DA_QK_DIM), kb) + bias)
        p2 = softmax2(_dot_nt(_head_mask(qb, 2 * h + 1, DA_QK_DIM), kb) + bias)
        o = _dot((p1 - lam * p2).astype(BF16), vb[:, h * DA_V_DIM:(h + 1) * DA_V_DIM])
        o_da.append(_rms(o, subw_ref[...]) * (1.0 - LAM_INIT))
    o_da = jnp.concatenate(o_da, axis=1)

    gq = nat[:, 512:768].astype(BF16).astype(F32)
    gk = nat[:, 768:1024].astype(BF16).astype(F32)
    gv = nat[:, 1024:1536].astype(BF16)
    gr = nat[:, 1536:2048].astype(BF16).astype(F32)
    g16 = nat[:, 2048:NAT_COLS].astype(BF16)
    la = _log_sigmoid(_dot(g16, gw_ref[...]) + gb_ref[...]) / GLA_GATE_TAU
    tri = (jj <= ii).astype(BF16)
    b = _cumsum_rows(tri, la)
    bl = b[m - 1:m]
    qd = (gq * GLA_Q_SCALE * jnp.exp(b)).astype(BF16)
    ki = (gk * jnp.exp(-b)).astype(BF16)
    kd = (gk * jnp.exp(bl - b)).astype(BF16)
    o_gla = []
    for h in range(GLA_HEADS):
        a = _dot_nt(_head_mask(qd, h, GLA_K_DIM), ki)
        a = jnp.where(jj <= ii, a, 0.0).astype(BF16)
        o = _dot(a, gv[:, h * GLA_V_DIM:(h + 1) * GLA_V_DIM])
        o_gla.append(_rms(o, gnw_ref[...]))
    o_gla = jnp.concatenate(o_gla, axis=1) * _silu(gr)
    st_out[...] = smask_ref[...] * _dot_tn(gv, kd)

    mix = jnp.concatenate([o_da, o_gla], axis=1).astype(BF16)
    h1 = x + _dot(mix, wout_ref[...])
    u = _dot(_rms(h1, n2_ref[...]).astype(BF16), wup_ref[...])
    ut_out[...] = u[m - 8:m]


def _proj_kernel(x_ref, n1_ref, wnat_ref, wt_ref, gw_ref, gb_ref,
                 k_out, qt_out, vt_out, gq_out, gk_out, gv_out, gr_out, la_out):
    ub = _rms(x_ref[0], n1_ref[...]).astype(BF16)
    nat = _dot(ub, wnat_ref[...])
    k_out[0] = nat[:, 0:512].astype(BF16)
    gq_out[0] = nat[:, 512:768].astype(BF16)
    gk_out[0] = nat[:, 768:1024].astype(BF16)
    gv_out[0] = nat[:, 1024:1536].astype(BF16)
    gr_out[0] = nat[:, 1536:2048].astype(BF16)
    g16 = nat[:, 2048:NAT_COLS].astype(BF16)
    la_out[0] = _log_sigmoid(_dot(g16, gw_ref[...]) + gb_ref[...]) / GLA_GATE_TAU
    tt = _dot_nt(wt_ref[...], ub)
    qt_out[0] = (tt[0:512] * Q_SCALE).astype(BF16)
    vt_out[0] = tt[512:1024].astype(BF16)


def _attn_kernel(c0_ref, lam4_ref, qt_ref, k_ref, vt_ref, augk_ref, augq_ref, bdiag_ref,
                 kmeta_ref, vtmeta_ref, bmeta_ref, subw_ref, o_ref,
                 qrhs_ref, m_ref, l_ref, acc_ref):
    h = pl.program_id(1)
    qi = pl.program_id(2)
    c0 = c0_ref[h]
    q0 = qi * Q_BLOCK

    qt = qt_ref[0]
    zeros = jnp.zeros((DA_QK_DIM, Q_BLOCK), BF16)
    qrhs_ref[0:64, 0:128] = qt[0:64]
    qrhs_ref[64:128, 0:128] = zeros
    qrhs_ref[0:64, 128:256] = zeros
    qrhs_ref[64:128, 128:256] = qt[64:128]
    qrhs_ref[128:256, 0:128] = augq_ref[0]
    qrhs_ref[128:256, 128:256] = augq_ref[0]

    m_ref[...] = jnp.full(m_ref.shape, -jnp.inf, F32)
    l_ref[...] = jnp.zeros(l_ref.shape, F32)
    acc_ref[...] = jnp.zeros(acc_ref.shape, F32)

    def step(keys, aug, vt, c, bias):
        s = _dot(jnp.concatenate([keys, aug], axis=1), qrhs_ref[...])
        if bias is not None:
            s = s + bias
        m_old = m_ref[...]
        m_new = jnp.maximum(m_old, jnp.max(s, axis=0, keepdims=True) + c)
        alpha = jnp.exp2(m_old - m_new)
        p = jnp.exp2(s - (m_new - c))
        l_ref[...] = alpha * l_ref[...] + jnp.sum(p, axis=0, keepdims=True)
        acc_ref[...] = alpha * acc_ref[...] + _dot(vt, p.astype(BF16))
        m_ref[...] = m_new

    aug128 = augk_ref[0, 0:Q_BLOCK, :]

    step(kmeta_ref[0], aug128, vtmeta_ref[0], -c0 * (q0 + N_META).astype(F32), bmeta_ref[...])

    n_full = (qi * Q_BLOCK) // KEY_TILE

    def full_body(j, carry):
        k0 = pl.multiple_of(j * KEY_TILE, KEY_TILE)
        step(k_ref[0, pl.ds(k0, KEY_TILE), :], augk_ref[0], vt_ref[0, :, pl.ds(k0, KEY_TILE)],
             -c0 * (q0 - k0).astype(F32), None)
        return carry

    lax.fori_loop(0, n_full, full_body, 0)

    def part_body(j, carry):
        k0 = pl.multiple_of(n_full * KEY_TILE + j * Q_BLOCK, Q_BLOCK)
        step(k_ref[0, pl.ds(k0, Q_BLOCK), :], aug128, vt_ref[0, :, pl.ds(k0, Q_BLOCK)],
             -c0 * (q0 - k0).astype(F32), None)
        return carry

    lax.fori_loop(0, qi - n_full * (KEY_TILE // Q_BLOCK), part_body, 0)

    kd0 = pl.multiple_of(q0, Q_BLOCK)
    step(k_ref[0, pl.ds(kd0, Q_BLOCK), :], aug128, vt_ref[0, :, pl.ds(kd0, Q_BLOCK)],
         jnp.float32(0.0), bdiag_ref[0])

    inv = 1.0 / l_ref[...]
    acc = acc_ref[...]
    o = acc[:, 0:128] * inv[:, 0:128] - _lam(lam4_ref[...]) * (acc[:, 128:256] * inv[:, 128:256])
    y = o * lax.rsqrt(jnp.mean(o * o, axis=0, keepdims=True) + RMS_EPS) * subw_ref[...] * (1.0 - LAM_INIT)
    o_ref[0] = y.T.astype(BF16)


def _gla_kernel(gq_ref, gk_ref, gv_ref, gr_ref, la_ref, tri_ref, st0_ref, gnw_ref, smask_ref,
                o_ref, st_ref):
    @pl.when(pl.program_id(1) == 0)
    def _():
        st_ref[...] = st0_ref[...]

    b = _cumsum_rows(tri_ref[...], la_ref[0])
    ci = lax.broadcasted_iota(jnp.int32, (GLA_HEADS * CHUNK, CHUNK), 0) % CHUNK
    si = lax.broadcasted_iota(jnp.int32, (GLA_HEADS * CHUNK, CHUNK), 1)
    causal = si <= ci
    smask = smask_ref[...]
    gnw = gnw_ref[...]

    for c in range(GLA_TILE // CHUNK):
        sl = slice(c * CHUNK, (c + 1) * CHUNK)
        bc = b[sl]
        bl = bc[CHUNK - 1:CHUNK]
        q = gq_ref[0, sl, :].astype(F32)
        k = gk_ref[0, sl, :].astype(F32)
        v = gv_ref[0, sl, :]
        qd = (q * GLA_Q_SCALE * jnp.exp(bc)).astype(BF16)
        ki = (k * jnp.exp(-bc)).astype(BF16)
        kd = (k * jnp.exp(bl - bc)).astype(BF16)
        qm = jnp.concatenate([_head_mask(qd, hh, GLA_K_DIM) for hh in range(GLA_HEADS)], axis=0)
        a = jnp.where(causal, _dot_nt(qm, ki), 0.0).astype(BF16)
        st = st_ref[...]
        o_inter = _dot_nt(qd, st.astype(BF16))
        r = gr_ref[0, sl, :].astype(F32)
        for hh in range(GLA_HEADS):
            vs = slice(hh * GLA_V_DIM, (hh + 1) * GLA_V_DIM)
            o = _dot(a[hh * CHUNK:(hh + 1) * CHUNK], v[:, vs]) + o_inter[:, vs]
            o_ref[0, sl, vs] = (_rms(o, gnw) * _silu(r[:, vs])).astype(BF16)
        st_ref[...] = jnp.exp(bl) * st + smask * _dot_tn(v, kd)


def _mlp_kernel(x_ref, oda_ref, ogla_ref, wout_ref, n2_ref, wup_ref, cw_ref, cb_ref, wdown_ref,
                fnw_ref, ut_ref, o_ref, carry_ref):
    @pl.when(pl.program_id(1) == 0)
    def _():
        carry_ref[...] = ut_ref[...]

    t = x_ref.shape[1]
    mix = jnp.concatenate([oda_ref[0], ogla_ref[0]], axis=1)
    h1 = x_ref[0] + _dot(mix, wout_ref[...])
    xn = _rms(h1, n2_ref[...]).astype(BF16)
    row = lax.broadcasted_iota(jnp.int32, (t, FF_CHUNK), 0)

    def conv(col0):
        cols = slice(col0, col0 + FF_CHUNK)
        u = _dot(xn, wup_ref[:, cols])
        p2 = carry_ref[6:7, cols]
        p1 = carry_ref[7:8, cols]
        u1 = jnp.where(row == 0, p1, pltpu.roll(u, 1, 0))
        u2 = jnp.where(row == 0, p2, jnp.where(row == 1, p1, pltpu.roll(u, 2, 0)))
        carry_ref[:, cols] = u[t - 8:t]
        w = cw_ref[:, cols]
        return w[0:1] * u2 + w[1:2] * u1 + w[2:3] * u + cb_ref[:, cols]

    acc = h1
    for j in range(D_FF // FF_CHUNK):
        val = conv(j * FF_CHUNK)
        gate = conv(D_FF + j * FF_CHUNK)
        act = (val * _silu(gate)).astype(BF16)
        acc = acc + _dot(act, wdown_ref[j * FF_CHUNK:(j + 1) * FF_CHUNK, :])
    o_ref[0] = _rms(acc, fnw_ref[...])


def _bf16_pieces(c):
    out = []
    for _ in range(3):
        p = np.float32(c).astype(BF16).astype(np.float32)
        out.append(p)
        c = np.float32(c) - p
    return out


@functools.lru_cache(maxsize=None)
def _attn_constants():
    slopes = 2.0 ** (-8.0 * np.arange(1, DA_HEADS + 1, dtype=np.float64) / DA_HEADS)
    c0 = (slopes * LOG2E).astype(np.float32)
    augk = np.zeros((DA_HEADS, KEY_TILE, LANE), np.float32)
    augq = np.zeros((DA_HEADS, LANE, Q_BLOCK), np.float32)
    j = np.arange(KEY_TILE)
    i = np.arange(Q_BLOCK)
    for h in range(DA_HEADS):
        a = _bf16_pieces(c0[h])
        for n in range(3):
            augk[h, :, n] = -a[n]
            augk[h, :, 3 + n] = j % 256
            augk[h, :, 6 + n] = (j // 256) * 256
            augq[h, n, :] = i
            augq[h, 3 + n, :] = a[n]
            augq[h, 6 + n, :] = a[n]
    jj, ii = np.meshgrid(np.arange(Q_BLOCK), np.arange(Q_BLOCK), indexing="ij")
    visible = (jj < CHUNK) | (ii >= CHUNK)
    bdiag = np.zeros((DA_HEADS, Q_BLOCK, 2 * Q_BLOCK), np.float32)
    for h in range(DA_HEADS):
        corr = np.where(jj > ii, -2.0 * c0[h] * (jj - ii), 0.0)
        tile = np.where(visible, corr, -np.inf).astype(np.float32)
        bdiag[h] = np.concatenate([tile, tile], axis=1)
    bmeta = np.where(np.arange(Q_BLOCK)[:, None] < N_META, 0.0, -np.inf).astype(np.float32)
    bmeta = np.broadcast_to(bmeta, (Q_BLOCK, 2 * Q_BLOCK)).copy()
    return c0, augk, augq, bdiag, bmeta


@functools.lru_cache(maxsize=None)
def _gla_constants():
    r = np.arange(GLA_TILE)
    tri = ((r[:, None] // CHUNK == r[None, :] // CHUNK) & (r[None, :] <= r[:, None])).astype(np.float32)
    rows = np.arange(GLA_WIDTH)[:, None] // GLA_V_DIM
    cols = np.arange(GLA_KW)[None, :] // GLA_K_DIM
    smask = (rows == cols).astype(np.float32)
    return tri, smask


def _full(shape):
    return pl.BlockSpec(shape, lambda *_: (0,) * len(shape))


def _resident(shape):
    return pl.BlockSpec(shape, lambda *_: (0,) * len(shape), pipeline_mode=pl.Buffered(1))


def _params(sem):
    return pltpu.CompilerParams(dimension_semantics=sem, vmem_limit_bytes=VMEM_LIMIT)


def kernel(x, meta_tokens, norm1_w, w_in, lambda_q1, lambda_k1, lambda_q2, lambda_k2, da_subln_w,
           gla_gate_w, gla_gate_b, gla_norm_w, w_out, norm2_w, w_up, conv_w, conv_b, w_down,
           final_norm_w):
    bsz, seq, _ = x.shape
    assert seq % KEY_TILE == 0 and seq % TOK_TILE == 0 and seq % GLA_TILE == 0
    assert norm1_w.shape[0] == 1
    nq = seq // Q_BLOCK

    w = w_in[0]
    wq, wk, wv = w[:, 0:512], w[:, 512:1024], w[:, 1024:1536]
    wrest, wlr = w[:, 1536:3072], w[:, 3072:3088]
    wnat = jnp.concatenate([wk, wrest, jnp.pad(wlr, ((0, 0), (0, LANE - GLA_GATE_RANK)))], axis=1).astype(BF16)
    wt = jnp.concatenate([wq, wv], axis=1).T.astype(BF16)
    gw = jnp.pad(gla_gate_w[0], ((0, LANE - GLA_GATE_RANK), (0, 0))).astype(BF16)
    gb = gla_gate_b[0][None].astype(F32)
    n1 = norm1_w[0][None]
    n2 = norm2_w[0][None]
    fnw = final_norm_w[None]
    gnw = gla_norm_w[0][None]
    subw_row = da_subln_w[0][None]
    subw_col = jnp.broadcast_to(da_subln_w[0][:, None], (DA_V_DIM, Q_BLOCK))
    lam4 = jnp.concatenate([lambda_q1, lambda_k1, lambda_q2, lambda_k2], axis=0)
    wout = w_out[0].astype(BF16)
    wup = w_up[0].astype(BF16)
    wdown = w_down[0].astype(BF16)
    cw = conv_w[0]
    cb = conv_b[0][None]

    c0_np, augk_np, augq_np, bdiag_np, bmeta_np = _attn_constants()
    tri_np, smask_np = _gla_constants()
    c0 = jnp.asarray(c0_np)
    augk = jnp.asarray(augk_np, BF16)
    augq = jnp.asarray(augq_np, BF16)
    bdiag = jnp.asarray(bdiag_np)
    bmeta = jnp.asarray(bmeta_np)
    tri = jnp.asarray(tri_np, BF16)
    smask = jnp.asarray(smask_np)

    smem = pl.BlockSpec(memory_space=pltpu.SMEM)
    vmem = pl.BlockSpec(memory_space=pltpu.VMEM)

    k_meta, v_meta, st0, utail = pl.pallas_call(
        _meta_kernel,
        out_shape=(jax.ShapeDtypeStruct((N_META, 512), BF16),
                   jax.ShapeDtypeStruct((N_META, 512), BF16),
                   jax.ShapeDtypeStruct((GLA_WIDTH, GLA_KW), F32),
                   jax.ShapeDtypeStruct((8, 2 * D_FF), F32)),
        in_specs=[vmem, vmem, vmem, vmem, vmem, vmem, vmem, smem, vmem, vmem, vmem, vmem, vmem, vmem],
        out_specs=(vmem, vmem, vmem, vmem),
        compiler_params=pltpu.CompilerParams(vmem_limit_bytes=VMEM_LIMIT),
        name="meta_mixer",
    )(meta_tokens, n1, wnat, wt, gw, gb, lam4, c0, subw_row, gnw, smask, wout, n2, wup)
    kmeta = jnp.pad(k_meta.reshape(N_META, DA_HEADS, LANE).transpose(1, 0, 2),
                    ((0, 0), (0, Q_BLOCK - N_META), (0, 0)))
    vtmeta = jnp.pad(v_meta.reshape(N_META, DA_HEADS, DA_V_DIM).transpose(1, 2, 0),
                     ((0, 0), (0, 0), (0, Q_BLOCK - N_META)))

    nt = seq // TOK_TILE
    tok = lambda width: pl.BlockSpec((1, TOK_TILE, width), lambda b, t: (b, t, 0))
    tokt = pl.BlockSpec((1, 512, TOK_TILE), lambda b, t: (b, 0, t))
    kcat, qt, vt, gq, gk, gv, gr, la = pl.pallas_call(
        _proj_kernel,
        grid=(bsz, nt),
        out_shape=(jax.ShapeDtypeStruct((bsz, seq, 512), BF16),
                   jax.ShapeDtypeStruct((bsz, 512, seq), BF16),
                   jax.ShapeDtypeStruct((bsz, 512, seq), BF16),
                   jax.ShapeDtypeStruct((bsz, seq, GLA_KW), BF16),
                   jax.ShapeDtypeStruct((bsz, seq, GLA_KW), BF16),
                   jax.ShapeDtypeStruct((bsz, seq, GLA_WIDTH), BF16),
                   jax.ShapeDtypeStruct((bsz, seq, GLA_WIDTH), BF16),
                   jax.ShapeDtypeStruct((bsz, seq, GLA_KW), F32)),
        in_specs=[tok(D_MODEL), _full((1, D_MODEL)), _resident((D_MODEL, NAT_COLS)),
                  _resident((1024, D_MODEL)), _full((LANE, GLA_KW)), _full((1, GLA_KW))],
        out_specs=(tok(512), tokt, tokt, tok(GLA_KW), tok(GLA_KW), tok(GLA_WIDTH), tok(GLA_WIDTH),
                   tok(GLA_KW)),
        compiler_params=_params(("parallel", "arbitrary")),
        name="in_proj",
    )(x, n1, wnat, wt, gw, gb)

    per_head = lambda shape: pl.BlockSpec((1,) + shape, lambda b, h, q: (h, 0, 0))
    o_da = pl.pallas_call(
        _attn_kernel,
        grid=(bsz, DA_HEADS, nq),
        out_shape=jax.ShapeDtypeStruct((bsz, seq, DA_WIDTH), BF16),
        in_specs=[smem, _full((4, DA_QK_DIM)),
                  pl.BlockSpec((1, 2 * DA_QK_DIM, Q_BLOCK), lambda b, h, q: (b, h, q)),
                  pl.BlockSpec((1, seq, LANE), lambda b, h, q: (b, 0, h)),
                  pl.BlockSpec((1, DA_V_DIM, seq), lambda b, h, q: (b, h, 0)),
                  per_head((KEY_TILE, LANE)), per_head((LANE, Q_BLOCK)), per_head((Q_BLOCK, 2 * Q_BLOCK)),
                  per_head((Q_BLOCK, LANE)), per_head((DA_V_DIM, Q_BLOCK)),
                  _full((Q_BLOCK, 2 * Q_BLOCK)), _full((DA_V_DIM, Q_BLOCK))],
        out_specs=pl.BlockSpec((1, Q_BLOCK, DA_V_DIM), lambda b, h, q: (b, q, h)),
        scratch_shapes=[pltpu.VMEM((2 * LANE, 2 * Q_BLOCK), BF16),
                        pltpu.VMEM((1, 2 * Q_BLOCK), F32),
                        pltpu.VMEM((1, 2 * Q_BLOCK), F32),
                        pltpu.VMEM((DA_V_DIM, 2 * Q_BLOCK), F32)],
        compiler_params=_params(("parallel", "parallel", "arbitrary")),
        name="diff_attn",
    )(c0, lam4, qt, kcat, vt, augk, augq, bdiag, kmeta, vtmeta, bmeta, subw_col)

    ng = seq // GLA_TILE
    gtok = lambda width: pl.BlockSpec((1, GLA_TILE, width), lambda b, t: (b, t, 0))
    o_gla = pl.pallas_call(
        _gla_kernel,
        grid=(bsz, ng),
        out_shape=jax.ShapeDtypeStruct((bsz, seq, GLA_WIDTH), BF16),
        in_specs=[gtok(GLA_KW), gtok(GLA_KW), gtok(GLA_WIDTH), gtok(GLA_WIDTH), gtok(GLA_KW),
                  _full((GLA_TILE, GLA_TILE)), _full((GLA_WIDTH, GLA_KW)), _full((1, GLA_V_DIM)),
                  _full((GLA_WIDTH, GLA_KW))],
        out_specs=gtok(GLA_WIDTH),
        scratch_shapes=[pltpu.VMEM((GLA_WIDTH, GLA_KW), F32)],
        compiler_params=_params(("parallel", "arbitrary")),
        name="gla",
    )(gq, gk, gv, gr, la, tri, st0, gnw, smask)

    out = pl.pallas_call(
        _mlp_kernel,
        grid=(bsz, nt),
        out_shape=jax.ShapeDtypeStruct((bsz, seq, D_MODEL), F32),
        in_specs=[tok(D_MODEL), tok(DA_WIDTH), tok(GLA_WIDTH), _resident((D_MODEL, D_MODEL)),
                  _full((1, D_MODEL)), _resident((D_MODEL, 2 * D_FF)), _full((3, 2 * D_FF)),
                  _full((1, 2 * D_FF)), _resident((D_FF, D_MODEL)), _full((1, D_MODEL)),
                  _full((8, 2 * D_FF))],
        out_specs=tok(D_MODEL),
        scratch_shapes=[pltpu.VMEM((8, 2 * D_FF), F32)],
        compiler_params=_params(("parallel", "arbitrary")),
        name="out_proj_mlp",
    )(x, o_da, o_gla, wout, n2, wup, cw, cb, wdown, fnw, utail)
    return out
```

```python
import functools
import math

import numpy as np
import jax
import jax.numpy as jnp
from jax import lax
from jax.experimental import pallas as pl
from jax.experimental.pallas import tpu as pltpu

F32 = jnp.float32
BF16 = jnp.bfloat16

D_MODEL = 1024
N_META = 16
CHUNK = 64
Q_SUPER = 512
DA_HEADS = 4
DA_QK_DIM = 64
DA_V_DIM = 128
DA_WIDTH = DA_HEADS * DA_V_DIM
GLA_HEADS = 4
GLA_K_DIM = 64
GLA_V_DIM = 128
GLA_KW = GLA_HEADS * GLA_K_DIM
GLA_WIDTH = GLA_HEADS * GLA_V_DIM
GLA_GATE_RANK = 16
GLA_GATE_TAU = 16.0
D_FF = 2816
RMS_EPS = 1e-6
LAM_INIT = 0.8 - 0.6 * math.exp(-0.3 * 0)

LOG2E = 1.4426950408889634
Q_SCALE = DA_QK_DIM ** -0.5 * LOG2E
GLA_Q_SCALE = GLA_K_DIM ** -0.5
LANE = 128
KEY_TILE = 512
SCORE_STRIP = 256
GLA_TILE = 512
TOK_TILE = 512
FF_CHUNK = 256
CONV_SLOTS = 4
DOWN_GROUP = 3
NAT_COLS = 2048 + LANE
VMEM_LIMIT = 56 * 1024 * 1024

_NT = (((1,), (1,)), ((), ()))
_TN = (((0,), (0,)), ((), ()))


def _dot(a, b):
    return jnp.dot(a, b, preferred_element_type=F32)


def _dot_nt(a, b):
    return lax.dot_general(a, b, _NT, preferred_element_type=F32)


def _dot_tn(a, b):
    return lax.dot_general(a, b, _TN, preferred_element_type=F32)


def _rms(x, w):
    return x * lax.rsqrt(jnp.mean(x * x, axis=-1, keepdims=True) + RMS_EPS) * w


def _log_sigmoid(z):
    return jnp.minimum(z, 0.0) - jnp.log(1.0 + jnp.exp(-jnp.abs(z)))


def _silu(g):
    hg = 0.5 * g
    return hg * jnp.tanh(hg) + hg


def _split3(x):
    hi = x.astype(BF16)
    r = x - hi.astype(F32)
    mid = r.astype(BF16)
    lo = (r - mid.astype(F32)).astype(BF16)
    return hi, mid, lo


def _cumsum_rows(tri, x):
    hi, mid, lo = _split3(x)
    return _dot(tri, hi) + _dot(tri, mid) + _dot(tri, lo)


def _lam(lam4):
    s1 = jnp.sum(lam4[0:1] * lam4[1:2], axis=-1, keepdims=True)
    s2 = jnp.sum(lam4[2:3] * lam4[3:4], axis=-1, keepdims=True)
    return jnp.exp(s1) - jnp.exp(s2) + LAM_INIT


def _head_mask(x, h, width):
    lane = lax.broadcasted_iota(jnp.int32, x.shape, 1)
    return jnp.where((lane >= h * width) & (lane < (h + 1) * width), x, jnp.zeros_like(x))


def _meta_kernel(meta_ref, n1_ref, wnat_ref, wt_ref, gw_ref, gb_ref, lam4_ref, c0_ref, subw_ref,
                 gnw_ref, smask_ref, wout_ref, n2_ref, wup_ref,
                 k_out, v_out, st_out, ut_out):
    m = N_META
    x = meta_ref[...]
    ub = _rms(x, n1_ref[...]).astype(BF16)
    nat = _dot(ub, wnat_ref[...])
    qv = _dot_nt(ub, wt_ref[...])
    kb = nat[:, 0:512].astype(BF16)
    qb = (qv[:, 0:512] * Q_SCALE).astype(BF16)
    vb = qv[:, 512:1024].astype(BF16)
    k_out[...] = kb
    v_out[...] = vb

    lam = _lam(lam4_ref[...])
    ii = lax.broadcasted_iota(jnp.int32, (m, m), 0)
    jj = lax.broadcasted_iota(jnp.int32, (m, m), 1)
    dist = jnp.abs(ii - jj).astype(F32)

    def softmax2(s):
        p = jnp.exp2(s - jnp.max(s, axis=-1, keepdims=True))
        return p / jnp.sum(p, axis=-1, keepdims=True)

    o_da = []
    for h in range(DA_HEADS):
        bias = -c0_ref[h] * dist
        p1 = softmax2(_dot_nt(_head_mask(qb, 2 * h, DA_QK_DIM), kb) + bias)
        p2 = softmax2(_dot_nt(_head_mask(qb, 2 * h + 1, DA_QK_DIM), kb) + bias)
        o = _dot((p1 - lam * p2).astype(BF16), vb[:, h * DA_V_DIM:(h + 1) * DA_V_DIM])
        o_da.append(_rms(o, subw_ref[...]) * (1.0 - LAM_INIT))
    o_da = jnp.concatenate(o_da, axis=1)

    gq = nat[:, 512:768].astype(BF16).astype(F32)
    gk = nat[:, 768:1024].astype(BF16).astype(F32)
    gv = nat[:, 1024:1536].astype(BF16)
    gr = nat[:, 1536:2048].astype(BF16).astype(F32)
    g16 = nat[:, 2048:NAT_COLS].astype(BF16)
    la = _log_sigmoid(_dot(g16, gw_ref[...]) + gb_ref[...]) / GLA_GATE_TAU
    tri = (jj <= ii).astype(BF16)
    b = _cumsum_rows(tri, la)
    bl = b[m - 1:m]
    qd = (gq * GLA_Q_SCALE * jnp.exp(b)).astype(BF16)
    ki = (gk * jnp.exp(-b)).astype(BF16)
    kd = (gk * jnp.exp(bl - b)).astype(BF16)
    o_gla = []
    for h in range(GLA_HEADS):
        a = _dot_nt(_head_mask(qd, h, GLA_K_DIM), ki)
        a = jnp.where(jj <= ii, a, 0.0).astype(BF16)
        o = _dot(a, gv[:, h * GLA_V_DIM:(h + 1) * GLA_V_DIM])
        o_gla.append(_rms(o, gnw_ref[...]))
    o_gla = jnp.concatenate(o_gla, axis=1) * _silu(gr)
    st_out[...] = smask_ref[...] * _dot_tn(gv, kd)

    mix = jnp.concatenate([o_da, o_gla], axis=1).astype(BF16)
    h1 = x + _dot(mix, wout_ref[...])
    u = _dot(_rms(h1, n2_ref[...]).astype(BF16), wup_ref[...])
    ut_out[...] = u[m - 8:m]


def _proj_kernel(x_ref, n1_ref, wnat_ref, wt_ref, gw_ref, gb_ref,
                 k_out, qt_out, vt_out, gq_out, gk_out, gv_out, gr_out, la_out):
    ub = _rms(x_ref[0], n1_ref[...]).astype(BF16)
    nat = _dot(ub, wnat_ref[...])
    k_out[0] = nat[:, 0:512].astype(BF16)
    gq_out[0] = nat[:, 512:768].astype(BF16)
    gk_out[0] = nat[:, 768:1024].astype(BF16)
    gv_out[0] = nat[:, 1024:1536].astype(BF16)
    gr_out[0] = nat[:, 1536:2048].astype(BF16)
    g16 = nat[:, 2048:NAT_COLS].astype(BF16)
    la_out[0] = _log_sigmoid(_dot(g16, gw_ref[...]) + gb_ref[...]) / GLA_GATE_TAU
    tt = _dot_nt(wt_ref[...], ub)
    qt_out[0] = (tt[0:512] * Q_SCALE).astype(BF16)
    vt_out[0] = tt[512:1024].astype(BF16)


def _attn_kernel(c0_ref, lam4_ref, qt_ref, k_ref, vt_ref, augk_ref, augq_ref, bdiag_ref,
                 kmeta_ref, vtmeta_ref, bmeta_ref, subw_ref, o_ref,
                 qrhs_ref, m_ref, l_ref, acc_ref, sa_ref, sb_ref):
    h = pl.program_id(1)
    qs = pl.program_id(2)
    c0 = c0_ref[h]
    q0 = qs * Q_SUPER

    qt = qt_ref[0]
    zeros = jnp.zeros((DA_QK_DIM, Q_SUPER), BF16)
    qrhs_ref[0:64, 0:Q_SUPER] = qt[0:64]
    qrhs_ref[64:128, 0:Q_SUPER] = zeros
    qrhs_ref[0:64, Q_SUPER:2 * Q_SUPER] = zeros
    qrhs_ref[64:128, Q_SUPER:2 * Q_SUPER] = qt[64:128]
    qrhs_ref[128:256, 0:Q_SUPER] = augq_ref[0]
    qrhs_ref[128:256, Q_SUPER:2 * Q_SUPER] = augq_ref[0]

    m_ref[...] = jnp.full(m_ref.shape, -jnp.inf, F32)
    l_ref[...] = jnp.zeros(l_ref.shape, F32)
    acc_ref[...] = jnp.zeros(acc_ref.shape, F32)

    def softmax_pv(s, vt, c, lanes):
        m_old = m_ref[:, lanes]
        m_new = jnp.maximum(m_old, jnp.max(s, axis=0, keepdims=True) + c)
        alpha = jnp.exp2(m_old - m_new)
        p = jnp.exp2(s - (m_new - c))
        l_ref[:, lanes] = alpha * l_ref[:, lanes] + jnp.sum(p, axis=0, keepdims=True)
        acc_ref[:, lanes] = alpha * acc_ref[:, lanes] + _dot(vt, p.astype(BF16))
        m_ref[:, lanes] = m_new

    strips = [slice(n * SCORE_STRIP, (n + 1) * SCORE_STRIP) for n in range(2 * Q_SUPER // SCORE_STRIP)]

    s_meta = _dot(jnp.concatenate([kmeta_ref[0], augk_ref[0, 0:LANE, :]], axis=1), qrhs_ref[...])
    s_meta = s_meta + bmeta_ref[...]
    for lanes in strips:
        softmax_pv(s_meta[:, lanes], vtmeta_ref[0], -c0 * (q0 + N_META).astype(F32), lanes)

    def scores_into(j, s_ref):
        k0 = pl.multiple_of(j * KEY_TILE, KEY_TILE)
        lhs = jnp.concatenate([k_ref[0, pl.ds(k0, KEY_TILE), :], augk_ref[0]], axis=1)
        s_ref[...] = _dot(lhs, qrhs_ref[...])

    def consume(j, s_ref, diag):
        k0 = pl.multiple_of(j * KEY_TILE, KEY_TILE)
        vt = vt_ref[0, :, pl.ds(k0, KEY_TILE)]
        c = -c0 * (q0 - k0).astype(F32)
        for lanes in strips:
            s = s_ref[:, lanes]
            if diag:
                s = s + bdiag_ref[0, :, lanes]
            softmax_pv(s, vt, c, lanes)

    scores_into(0, sa_ref)

    def pair_body(jj, carry):
        j = 2 * jj
        scores_into(j + 1, sb_ref)
        consume(j, sa_ref, False)
        scores_into(j + 2, sa_ref)
        consume(j + 1, sb_ref, False)
        return carry

    lax.fori_loop(0, qs // 2, pair_body, 0)

    @pl.when(qs % 2 == 0)
    def _():
        consume(qs, sa_ref, True)

    @pl.when(qs % 2 == 1)
    def _():
        scores_into(qs, sb_ref)
        consume(qs - 1, sa_ref, False)
        consume(qs, sb_ref, True)

    inv = 1.0 / l_ref[...]
    acc = acc_ref[...]
    o = (acc[:, 0:Q_SUPER] * inv[:, 0:Q_SUPER]
         - _lam(lam4_ref[...]) * (acc[:, Q_SUPER:2 * Q_SUPER] * inv[:, Q_SUPER:2 * Q_SUPER]))
    y = o * lax.rsqrt(jnp.mean(o * o, axis=0, keepdims=True) + RMS_EPS) * subw_ref[...] * (1.0 - LAM_INIT)
    o_ref[0] = y.T.astype(BF16)


def _gla_kernel(gq_ref, gk_ref, gv_ref, gr_ref, la_ref, tri_ref, st0_ref, gnw_ref, smask_ref,
                o_ref, st_ref):
    @pl.when(pl.program_id(1) == 0)
    def _():
        st_ref[...] = st0_ref[...]

    b = _cumsum_rows(tri_ref[...], la_ref[0])
    ci = lax.broadcasted_iota(jnp.int32, (GLA_HEADS * CHUNK, CHUNK), 0) % CHUNK
    si = lax.broadcasted_iota(jnp.int32, (GLA_HEADS * CHUNK, CHUNK), 1)
    causal = si <= ci
    smask = smask_ref[...]
    gnw = gnw_ref[...]

    for c in range(GLA_TILE // CHUNK):
        sl = slice(c * CHUNK, (c + 1) * CHUNK)
        bc = b[sl]
        bl = bc[CHUNK - 1:CHUNK]
        q = gq_ref[0, sl, :].astype(F32)
        k = gk_ref[0, sl, :].astype(F32)
        v = gv_ref[0, sl, :]
        qd = (q * GLA_Q_SCALE * jnp.exp(bc)).astype(BF16)
        ki = (k * jnp.exp(-bc)).astype(BF16)
        kd = (k * jnp.exp(bl - bc)).astype(BF16)
        qm = jnp.concatenate([_head_mask(qd, hh, GLA_K_DIM) for hh in range(GLA_HEADS)], axis=0)
        a = jnp.where(causal, _dot_nt(qm, ki), 0.0).astype(BF16)
        st = st_ref[...]
        o_inter = _dot_nt(qd, st.astype(BF16))
        r = gr_ref[0, sl, :].astype(F32)
        for hh in range(GLA_HEADS):
            vs = slice(hh * GLA_V_DIM, (hh + 1) * GLA_V_DIM)
            o = _dot(a[hh * CHUNK:(hh + 1) * CHUNK], v[:, vs]) + o_inter[:, vs]
            o_ref[0, sl, vs] = (_rms(o, gnw) * _silu(r[:, vs])).astype(BF16)
        st_ref[...] = jnp.exp(bl) * st + smask * _dot_tn(v, kd)


def _mlp_kernel(x_ref, oda_ref, ogla_ref, wout_ref, n2_ref, wup_ref, cw_ref, cb_ref, wdown_ref,
                fnw_ref, ut_ref, o_ref, carry_ref, ubuf_ref, act_ref):
    @pl.when(pl.program_id(1) == 0)
    def _():
        carry_ref[...] = ut_ref[...]

    t = x_ref.shape[1]
    mix = jnp.concatenate([oda_ref[0], ogla_ref[0]], axis=1)
    h1 = x_ref[0] + _dot(mix, wout_ref[...])
    xn = _rms(h1, n2_ref[...]).astype(BF16)

    def conv(col0, slot):
        cols = slice(col0, col0 + FF_CHUNK)
        u = _dot(xn, wup_ref[:, cols])
        ubuf_ref[slot, 0:8, :] = carry_ref[:, cols]
        ubuf_ref[slot, 8:8 + t, :] = u
        carry_ref[:, cols] = u[t - 8:t]
        w = cw_ref[:, cols]
        return (w[0:1] * ubuf_ref[slot, 6:6 + t, :] + w[1:2] * ubuf_ref[slot, 7:7 + t, :]
                + w[2:3] * u + cb_ref[:, cols])

    acc = h1
    n_chunks = D_FF // FF_CHUNK
    for j in range(n_chunks):
        val = conv(j * FF_CHUNK, (2 * j) % CONV_SLOTS)
        gate = conv(D_FF + j * FF_CHUNK, (2 * j + 1) % CONV_SLOTS)
        act_ref[:, j * FF_CHUNK:(j + 1) * FF_CHUNK] = (val * _silu(gate)).astype(BF16)
        if j % DOWN_GROUP == DOWN_GROUP - 1 or j == n_chunks - 1:
            lo = (j // DOWN_GROUP) * DOWN_GROUP * FF_CHUNK
            hi = (j + 1) * FF_CHUNK
            acc = acc + _dot(act_ref[:, lo:hi], wdown_ref[lo:hi, :])
    o_ref[0] = _rms(acc, fnw_ref[...])


def _bf16_pieces(c):
    out = []
    for _ in range(3):
        p = np.float32(c).astype(BF16).astype(np.float32)
        out.append(p)
        c = np.float32(c) - p
    return out


@functools.lru_cache(maxsize=None)
def _attn_constants():
    assert KEY_TILE == Q_SUPER == 512
    slopes = 2.0 ** (-8.0 * np.arange(1, DA_HEADS + 1, dtype=np.float64) / DA_HEADS)
    c0 = (slopes * LOG2E).astype(np.float32)
    augk = np.zeros((DA_HEADS, KEY_TILE, LANE), np.float32)
    augq = np.zeros((DA_HEADS, LANE, Q_SUPER), np.float32)
    j = np.arange(KEY_TILE)
    iq = np.arange(Q_SUPER)
    for h in range(DA_HEADS):
        a = _bf16_pieces(c0[h])
        for n in range(3):
            augk[h, :, n] = -a[n]
            augq[h, n, :] = iq % 256
            augk[h, :, 3 + n] = -a[n]
            augq[h, 3 + n, :] = (iq // 256) * 256
            augk[h, :, 6 + n] = j % 256
            augq[h, 6 + n, :] = a[n]
            augk[h, :, 9 + n] = (j // 256) * 256
            augq[h, 9 + n, :] = a[n]
    jj, ii = np.meshgrid(np.arange(KEY_TILE), np.arange(Q_SUPER), indexing="ij")
    visible = (jj // CHUNK) <= (ii // CHUNK)
    bdiag = np.zeros((DA_HEADS, KEY_TILE, 2 * Q_SUPER), np.float32)
    for h in range(DA_HEADS):
        corr = np.where(jj > ii, -2.0 * c0[h] * (jj - ii), 0.0)
        tile = np.where(visible, corr, -np.inf).astype(np.float32)
        bdiag[h] = np.concatenate([tile, tile], axis=1)
    bmeta = np.where(np.arange(LANE)[:, None] < N_META, 0.0, -np.inf).astype(np.float32)
    bmeta = np.broadcast_to(bmeta, (LANE, 2 * Q_SUPER)).copy()
    return c0, augk, augq, bdiag, bmeta


@functools.lru_cache(maxsize=None)
def _gla_constants():
    r = np.arange(GLA_TILE)
    tri = ((r[:, None] // CHUNK == r[None, :] // CHUNK) & (r[None, :] <= r[:, None])).astype(np.float32)
    rows = np.arange(GLA_WIDTH)[:, None] // GLA_V_DIM
    cols = np.arange(GLA_KW)[None, :] // GLA_K_DIM
    smask = (rows == cols).astype(np.float32)
    return tri, smask


def _full(shape):
    return pl.BlockSpec(shape, lambda *_: (0,) * len(shape))


def _resident(shape):
    return pl.BlockSpec(shape, lambda *_: (0,) * len(shape), pipeline_mode=pl.Buffered(1))


def _params(sem):
    return pltpu.CompilerParams(dimension_semantics=sem, vmem_limit_bytes=VMEM_LIMIT)


def kernel(x, meta_tokens, norm1_w, w_in, lambda_q1, lambda_k1, lambda_q2, lambda_k2, da_subln_w,
           gla_gate_w, gla_gate_b, gla_norm_w, w_out, norm2_w, w_up, conv_w, conv_b, w_down,
           final_norm_w):
    bsz, seq, _ = x.shape
    assert seq % KEY_TILE == 0 and seq % TOK_TILE == 0 and seq % GLA_TILE == 0
    assert norm1_w.shape[0] == 1
    nq = seq // Q_SUPER

    w = w_in[0]
    wq, wk, wv = w[:, 0:512], w[:, 512:1024], w[:, 1024:1536]
    wrest, wlr = w[:, 1536:3072], w[:, 3072:3088]
    wnat = jnp.concatenate([wk, wrest, jnp.pad(wlr, ((0, 0), (0, LANE - GLA_GATE_RANK)))], axis=1).astype(BF16)
    wt = jnp.concatenate([wq, wv], axis=1).T.astype(BF16)
    gw = jnp.pad(gla_gate_w[0], ((0, LANE - GLA_GATE_RANK), (0, 0))).astype(BF16)
    gb = gla_gate_b[0][None].astype(F32)
    n1 = norm1_w[0][None]
    n2 = norm2_w[0][None]
    fnw = final_norm_w[None]
    gnw = gla_norm_w[0][None]
    subw_row = da_subln_w[0][None]
    subw_col = jnp.broadcast_to(da_subln_w[0][:, None], (DA_V_DIM, Q_SUPER))
    lam4 = jnp.concatenate([lambda_q1, lambda_k1, lambda_q2, lambda_k2], axis=0)
    wout = w_out[0].astype(BF16)
    wup = w_up[0].astype(BF16)
    wdown = w_down[0].astype(BF16)
    cw = conv_w[0]
    cb = conv_b[0][None]

    c0_np, augk_np, augq_np, bdiag_np, bmeta_np = _attn_constants()
    tri_np, smask_np = _gla_constants()
    c0 = jnp.asarray(c0_np)
    augk = jnp.asarray(augk_np, BF16)
    augq = jnp.asarray(augq_np, BF16)
    bdiag = jnp.asarray(bdiag_np)
    bmeta = jnp.asarray(bmeta_np)
    tri = jnp.asarray(tri_np, BF16)
    smask = jnp.asarray(smask_np)

    smem = pl.BlockSpec(memory_space=pltpu.SMEM)
    vmem = pl.BlockSpec(memory_space=pltpu.VMEM)

    k_meta, v_meta, st0, utail = pl.pallas_call(
        _meta_kernel,
        out_shape=(jax.ShapeDtypeStruct((N_META, 512), BF16),
                   jax.ShapeDtypeStruct((N_META, 512), BF16),
                   jax.ShapeDtypeStruct((GLA_WIDTH, GLA_KW), F32),
                   jax.ShapeDtypeStruct((8, 2 * D_FF), F32)),
        in_specs=[vmem, vmem, vmem, vmem, vmem, vmem, vmem, smem, vmem, vmem, vmem, vmem, vmem, vmem],
        out_specs=(vmem, vmem, vmem, vmem),
        compiler_params=pltpu.CompilerParams(vmem_limit_bytes=VMEM_LIMIT),
        name="meta_mixer",
    )(meta_tokens, n1, wnat, wt, gw, gb, lam4, c0, subw_row, gnw, smask, wout, n2, wup)
    kmeta = jnp.pad(k_meta.reshape(N_META, DA_HEADS, LANE).transpose(1, 0, 2),
                    ((0, 0), (0, LANE - N_META), (0, 0)))
    vtmeta = jnp.pad(v_meta.reshape(N_META, DA_HEADS, DA_V_DIM).transpose(1, 2, 0),
                     ((0, 0), (0, 0), (0, LANE - N_META)))

    nt = seq // TOK_TILE
    tok = lambda width: pl.BlockSpec((1, TOK_TILE, width), lambda b, t: (b, t, 0))
    tokt = pl.BlockSpec((1, 512, TOK_TILE), lambda b, t: (b, 0, t))
    kcat, qt, vt, gq, gk, gv, gr, la = pl.pallas_call(
        _proj_kernel,
        grid=(bsz, nt),
        out_shape=(jax.ShapeDtypeStruct((bsz, seq, 512), BF16),
                   jax.ShapeDtypeStruct((bsz, 512, seq), BF16),
                   jax.ShapeDtypeStruct((bsz, 512, seq), BF16),
                   jax.ShapeDtypeStruct((bsz, seq, GLA_KW), BF16),
                   jax.ShapeDtypeStruct((bsz, seq, GLA_KW), BF16),
                   jax.ShapeDtypeStruct((bsz, seq, GLA_WIDTH), BF16),
                   jax.ShapeDtypeStruct((bsz, seq, GLA_WIDTH), BF16),
                   jax.ShapeDtypeStruct((bsz, seq, GLA_KW), F32)),
        in_specs=[tok(D_MODEL), _full((1, D_MODEL)), _resident((D_MODEL, NAT_COLS)),
                  _resident((1024, D_MODEL)), _full((LANE, GLA_KW)), _full((1, GLA_KW))],
        out_specs=(tok(512), tokt, tokt, tok(GLA_KW), tok(GLA_KW), tok(GLA_WIDTH), tok(GLA_WIDTH),
                   tok(GLA_KW)),
        compiler_params=_params(("parallel", "arbitrary")),
        name="in_proj",
    )(x, n1, wnat, wt, gw, gb)

    per_head = lambda shape: pl.BlockSpec((1,) + shape, lambda b, h, q: (h, 0, 0))
    o_da = pl.pallas_call(
        _attn_kernel,
        grid=(bsz, DA_HEADS, nq),
        out_shape=jax.ShapeDtypeStruct((bsz, seq, DA_WIDTH), BF16),
        in_specs=[smem, _full((4, DA_QK_DIM)),
                  pl.BlockSpec((1, 2 * DA_QK_DIM, Q_SUPER), lambda b, h, q: (b, h, q)),
                  pl.BlockSpec((1, seq, LANE), lambda b, h, q: (b, 0, h)),
                  pl.BlockSpec((1, DA_V_DIM, seq), lambda b, h, q: (b, h, 0)),
                  per_head((KEY_TILE, LANE)), per_head((LANE, Q_SUPER)), per_head((KEY_TILE, 2 * Q_SUPER)),
                  per_head((LANE, LANE)), per_head((DA_V_DIM, LANE)),
                  _full((LANE, 2 * Q_SUPER)), _full((DA_V_DIM, Q_SUPER))],
        out_specs=pl.BlockSpec((1, Q_SUPER, DA_V_DIM), lambda b, h, q: (b, q, h)),
        scratch_shapes=[pltpu.VMEM((2 * LANE, 2 * Q_SUPER), BF16),
                        pltpu.VMEM((1, 2 * Q_SUPER), F32),
                        pltpu.VMEM((1, 2 * Q_SUPER), F32),
                        pltpu.VMEM((DA_V_DIM, 2 * Q_SUPER), F32),
                        pltpu.VMEM((KEY_TILE, 2 * Q_SUPER), F32),
                        pltpu.VMEM((KEY_TILE, 2 * Q_SUPER), F32)],
        compiler_params=_params(("parallel", "parallel", "arbitrary")),
        name="diff_attn",
    )(c0, lam4, qt, kcat, vt, augk, augq, bdiag, kmeta, vtmeta, bmeta, subw_col)

    ng = seq // GLA_TILE
    gtok = lambda width: pl.BlockSpec((1, GLA_TILE, width), lambda b, t: (b, t, 0))
    o_gla = pl.pallas_call(
        _gla_kernel,
        grid=(bsz, ng),
        out_shape=jax.ShapeDtypeStruct((bsz, seq, GLA_WIDTH), BF16),
        in_specs=[gtok(GLA_KW), gtok(GLA_KW), gtok(GLA_WIDTH), gtok(GLA_WIDTH), gtok(GLA_KW),
                  _full((GLA_TILE, GLA_TILE)), _full((GLA_WIDTH, GLA_KW)), _full((1, GLA_V_DIM)),
                  _full((GLA_WIDTH, GLA_KW))],
        out_specs=gtok(GLA_WIDTH),
        scratch_shapes=[pltpu.VMEM((GLA_WIDTH, GLA_KW), F32)],
        compiler_params=_params(("parallel", "arbitrary")),
        name="gla",
    )(gq, gk, gv, gr, la, tri, st0, gnw, smask)

    out = pl.pallas_call(
        _mlp_kernel,
        grid=(bsz, nt),
        out_shape=jax.ShapeDtypeStruct((bsz, seq, D_MODEL), F32),
        in_specs=[tok(D_MODEL), tok(DA_WIDTH), tok(GLA_WIDTH), _resident((D_MODEL, D_MODEL)),
                  _full((1, D_MODEL)), _resident((D_MODEL, 2 * D_FF)), _full((3, 2 * D_FF)),
                  _full((1, 2 * D_FF)), _resident((D_FF, D_MODEL)), _full((1, D_MODEL)),
                  _full((8, 2 * D_FF))],
        out_specs=tok(D_MODEL),
        scratch_shapes=[pltpu.VMEM((8, 2 * D_FF), F32),
                        pltpu.VMEM((CONV_SLOTS, TOK_TILE + 8, FF_CHUNK), F32),
                        pltpu.VMEM((TOK_TILE, D_FF), BF16)],
        compiler_params=_params(("parallel", "arbitrary")),
        name="out_proj_mlp",
    )(x, o_da, o_gla, wout, n2, wup, cw, cb, wdown, fnw, utail)
    return out
```

```python
import functools
import math

import numpy as np
import jax
import jax.numpy as jnp
from jax import lax
from jax.experimental import pallas as pl
from jax.experimental.pallas import tpu as pltpu

F32 = jnp.float32
BF16 = jnp.bfloat16

D_MODEL = 1024
N_META = 16
CHUNK = 64
Q_SUPER = 1024
DIAG_TILE = 512
DA_HEADS = 4
DA_QK_DIM = 64
DA_V_DIM = 128
DA_WIDTH = DA_HEADS * DA_V_DIM
GLA_HEADS = 4
GLA_K_DIM = 64
GLA_V_DIM = 128
GLA_KW = GLA_HEADS * GLA_K_DIM
GLA_WIDTH = GLA_HEADS * GLA_V_DIM
GLA_GATE_RANK = 16
GLA_GATE_TAU = 16.0
D_FF = 2816
RMS_EPS = 1e-6
LAM_INIT = 0.8 - 0.6 * math.exp(-0.3 * 0)

LOG2E = 1.4426950408889634
Q_SCALE = DA_QK_DIM ** -0.5 * LOG2E
GLA_Q_SCALE = GLA_K_DIM ** -0.5
LANE = 128
KEY_TILE = 512
SCORE_STRIP = 256
GLA_TILE = 512
CUMSUM_ROWS = 128
TOK_TILE = 512
MLP_TILE = 512
FF_CHUNK = 256
CONV_SLOTS = 4
DOWN_GROUP = 11
NAT_COLS = 2048 + LANE
VMEM_LIMIT = 56 * 1024 * 1024

_NT = (((1,), (1,)), ((), ()))
_TN = (((0,), (0,)), ((), ()))


def _dot(a, b):
    return jnp.dot(a, b, preferred_element_type=F32)


def _dot_nt(a, b):
    return lax.dot_general(a, b, _NT, preferred_element_type=F32)


def _dot_tn(a, b):
    return lax.dot_general(a, b, _TN, preferred_element_type=F32)


def _rms(x, w):
    return x * lax.rsqrt(jnp.mean(x * x, axis=-1, keepdims=True) + RMS_EPS) * w


def _log_sigmoid(z):
    return jnp.minimum(z, 0.0) - jnp.log(1.0 + jnp.exp(-jnp.abs(z)))


def _silu(g):
    hg = 0.5 * g
    return hg * jnp.tanh(hg) + hg


def _split3(x):
    hi = x.astype(BF16)
    r = x - hi.astype(F32)
    mid = r.astype(BF16)
    lo = (r - mid.astype(F32)).astype(BF16)
    return hi, mid, lo


def _cumsum_rows(tri, x):
    hi, mid, lo = _split3(x)
    return _dot(tri, hi) + _dot(tri, mid) + _dot(tri, lo)


def _lam(lam4):
    s1 = jnp.sum(lam4[0:1] * lam4[1:2], axis=-1, keepdims=True)
    s2 = jnp.sum(lam4[2:3] * lam4[3:4], axis=-1, keepdims=True)
    return jnp.exp(s1) - jnp.exp(s2) + LAM_INIT


def _head_mask(x, h, width):
    lane = lax.broadcasted_iota(jnp.int32, x.shape, 1)
    return jnp.where((lane >= h * width) & (lane < (h + 1) * width), x, jnp.zeros_like(x))


def _meta_kernel(meta_ref, n1_ref, wnat_ref, wt_ref, gw_ref, gb_ref, lam4_ref, c0_ref, subw_ref,
                 gnw_ref, smask_ref, wout_ref, n2_ref, wup_ref,
                 k_out, v_out, st_out, ut_out):
    m = N_META
    x = meta_ref[...]
    ub = _rms(x, n1_ref[...]).astype(BF16)
    nat = _dot(ub, wnat_ref[...])
    qv = _dot_nt(ub, wt_ref[...])
    kb = nat[:, 0:512].astype(BF16)
    qb = (qv[:, 0:512] * Q_SCALE).astype(BF16)
    vb = qv[:, 512:1024].astype(BF16)
    k_out[...] = kb
    v_out[...] = vb

    lam = _lam(lam4_ref[...])
    ii = lax.broadcasted_iota(jnp.int32, (m, m), 0)
    jj = lax.broadcasted_iota(jnp.int32, (m, m), 1)
    dist = jnp.abs(ii - jj).astype(F32)

    def softmax2(s):
        p = jnp.exp2(s - jnp.max(s, axis=-1, keepdims=True))
        return p / jnp.sum(p, axis=-1, keepdims=True)

    o_da = []
    for h in range(DA_HEADS):
        bias = -c0_ref[h] * dist
        p1 = softmax2(_dot_nt(_head_mask(qb, 2 * h, DA_QK_DIM), kb) + bias)
        p2 = softmax2(_dot_nt(_head_mask(qb, 2 * h + 1, DA_QK_DIM), kb) + bias)
        o = _dot((p1 - lam * p2).astype(BF16), vb[:, h * DA_V_DIM:(h + 1) * DA_V_DIM])
        o_da.append(_rms(o, subw_ref[...]) * (1.0 - LAM_INIT))
    o_da = jnp.concatenate(o_da, axis=1)

    gq = nat[:, 512:768].astype(BF16).astype(F32)
    gk = nat[:, 768:1024].astype(BF16).astype(F32)
    gv = nat[:, 1024:1536].astype(BF16)
    gr = nat[:, 1536:2048].astype(BF16).astype(F32)
    g16 = nat[:, 2048:NAT_COLS].astype(BF16)
    la = _log_sigmoid(_dot(g16, gw_ref[...]) + gb_ref[...]) / GLA_GATE_TAU
    tri = (jj <= ii).astype(BF16)
    b = _cumsum_rows(tri, la)
    bl = b[m - 1:m]
    qd = (gq * GLA_Q_SCALE * jnp.exp(b)).astype(BF16)
    ki = (gk * jnp.exp(-b)).astype(BF16)
    kd = (gk * jnp.exp(bl - b)).astype(BF16)
    o_gla = []
    for h in range(GLA_HEADS):
        a = _dot_nt(_head_mask(qd, h, GLA_K_DIM), ki)
        a = jnp.where(jj <= ii, a, 0.0).astype(BF16)
        o = _dot(a, gv[:, h * GLA_V_DIM:(h + 1) * GLA_V_DIM])
        o_gla.append(_rms(o, gnw_ref[...]))
    o_gla = jnp.concatenate(o_gla, axis=1) * _silu(gr)
    st_out[...] = smask_ref[...] * _dot_tn(gv, kd)

    mix = jnp.concatenate([o_da, o_gla], axis=1).astype(BF16)
    h1 = x + _dot(mix, wout_ref[...])
    u = _dot(_rms(h1, n2_ref[...]).astype(BF16), wup_ref[...])
    ut_out[...] = u[m - 8:m]


def _proj_kernel(x_ref, n1_ref, wnat_ref, wt_ref, gw_ref, gb_ref,
                 k_out, qt_out, vt_out, gq_out, gk_out, gv_out, gr_out, la_out):
    ub = _rms(x_ref[0], n1_ref[...]).astype(BF16)
    nat = _dot(ub, wnat_ref[...])
    k_out[0] = nat[:, 0:512].astype(BF16)
    gq_out[0] = nat[:, 512:768].astype(BF16)
    gk_out[0] = nat[:, 768:1024].astype(BF16)
    gv_out[0] = nat[:, 1024:1536].astype(BF16)
    gr_out[0] = nat[:, 1536:2048].astype(BF16)
    g16 = nat[:, 2048:NAT_COLS].astype(BF16)
    la_out[0] = _log_sigmoid(_dot(g16, gw_ref[...]) + gb_ref[...]) / GLA_GATE_TAU
    tt = _dot_nt(wt_ref[...], ub)
    qt_out[0] = (tt[0:512] * Q_SCALE).astype(BF16)
    vt_out[0] = tt[512:1024].astype(BF16)


def _attn_kernel(c0_ref, lam4_ref, qt_ref, k_ref, vt_ref, augk_ref, augq_ref, bdiag_ref,
                 kmeta_ref, vtmeta_ref, subw_ref, o_ref,
                 qrhs_ref, m_ref, l_ref, acc_ref, sa_ref, sb_ref, mxa_ref, mxb_ref):
    h = pl.program_id(1)
    qs = pl.program_id(2)
    c0 = c0_ref[h]
    q0 = qs * Q_SUPER

    qt = qt_ref[0]
    zeros = jnp.zeros((DA_QK_DIM, Q_SUPER), BF16)
    qrhs_ref[0:64, 0:Q_SUPER] = qt[0:64]
    qrhs_ref[64:128, 0:Q_SUPER] = zeros
    qrhs_ref[0:64, Q_SUPER:2 * Q_SUPER] = zeros
    qrhs_ref[64:128, Q_SUPER:2 * Q_SUPER] = qt[64:128]
    qrhs_ref[128:256, 0:Q_SUPER] = augq_ref[0]
    qrhs_ref[128:256, Q_SUPER:2 * Q_SUPER] = augq_ref[0]

    m_ref[...] = jnp.full(m_ref.shape, -jnp.inf, F32)
    l_ref[...] = jnp.zeros(l_ref.shape, F32)
    acc_ref[...] = jnp.zeros(acc_ref.shape, F32)

    def softmax_pv(s, mx, vt, c, lanes):
        m_old = m_ref[:, lanes]
        m_new = jnp.maximum(m_old, mx + c)
        alpha = jnp.exp2(m_old - m_new)
        p = jnp.exp2(s - (m_new - c))
        l_ref[:, lanes] = alpha * l_ref[:, lanes] + jnp.sum(p, axis=0, keepdims=True)
        pb = p.astype(BF16)
        if pb.shape[0] < vt.shape[1]:
            pb = jnp.concatenate([pb, jnp.zeros((vt.shape[1] - pb.shape[0], pb.shape[1]), BF16)], axis=0)
        acc_ref[:, lanes] = alpha * acc_ref[:, lanes] + _dot(vt, pb)
        m_ref[:, lanes] = m_new

    n_strips = 2 * Q_SUPER // SCORE_STRIP
    per_map = Q_SUPER // SCORE_STRIP
    strips = [slice(n * SCORE_STRIP, (n + 1) * SCORE_STRIP) for n in range(n_strips)]

    def query_block(n):
        return (n % per_map) * SCORE_STRIP // DIAG_TILE

    def scores_into(j, s_ref, mx_ref, strip_ids=None):
        k0 = pl.multiple_of(j * KEY_TILE, KEY_TILE)
        lhs = jnp.concatenate([k_ref[0, pl.ds(k0, KEY_TILE), :], augk_ref[0]], axis=1)
        if strip_ids is None:
            s = _dot(lhs, qrhs_ref[...])
            s_ref[...] = s
            mx_ref[...] = jnp.max(s, axis=0, keepdims=True)
        else:
            for n in strip_ids:
                s_ref[:, strips[n]] = _dot(lhs, qrhs_ref[:, strips[n]])

    def consume(j, s_ref, mx_ref, diag_block):
        k0 = pl.multiple_of(j * KEY_TILE, KEY_TILE)
        vt = vt_ref[0, :, pl.ds(k0, KEY_TILE)]
        c = -c0 * (q0 - k0).astype(F32)
        for n, lanes in enumerate(strips):
            if diag_block is None or query_block(n) > diag_block:
                softmax_pv(s_ref[:, lanes], mx_ref[:, lanes], vt, c, lanes)
            elif query_block(n) == diag_block:
                sub = (n % per_map) * SCORE_STRIP % DIAG_TILE
                rows = KEY_TILE // 2 if sub == 0 else KEY_TILE
                b0 = (n // per_map) * DIAG_TILE + sub
                blanes = slice(b0, b0 + SCORE_STRIP)
                s = s_ref[0:rows, lanes] + bdiag_ref[0, 0:rows, blanes]
                softmax_pv(s, jnp.max(s, axis=0, keepdims=True), vt[:, 0:rows], c, lanes)

    s_meta = _dot(jnp.concatenate([kmeta_ref[0], augk_ref[0, 0:N_META, :]], axis=1), qrhs_ref[...])
    scores_into(0, sa_ref, mxa_ref)
    for lanes in strips:
        sm = s_meta[:, lanes]
        softmax_pv(sm, jnp.max(sm, axis=0, keepdims=True), vtmeta_ref[0],
                   -c0 * (q0 + N_META).astype(F32), lanes)

    n_diag = Q_SUPER // KEY_TILE
    assert n_diag == 2 and KEY_TILE == DIAG_TILE
    n_past = qs * n_diag

    def pair_body(jj, carry):
        j = 2 * jj
        scores_into(j + 1, sb_ref, mxb_ref)
        consume(j, sa_ref, mxa_ref, None)
        scores_into(j + 2, sa_ref, mxa_ref)
        consume(j + 1, sb_ref, mxb_ref, None)
        return carry

    lax.fori_loop(0, qs, pair_body, 0)

    later = [n for n in range(n_strips) if query_block(n) == 1]
    scores_into(n_past + 1, sb_ref, mxb_ref, later)
    consume(n_past, sa_ref, mxa_ref, 0)
    consume(n_past + 1, sb_ref, mxb_ref, 1)

    inv = 1.0 / l_ref[...]
    acc = acc_ref[...]
    o = (acc[:, 0:Q_SUPER] * inv[:, 0:Q_SUPER]
         - _lam(lam4_ref[...]) * (acc[:, Q_SUPER:2 * Q_SUPER] * inv[:, Q_SUPER:2 * Q_SUPER]))
    y = o * lax.rsqrt(jnp.mean(o * o, axis=0, keepdims=True) + RMS_EPS) * subw_ref[...] * (1.0 - LAM_INIT)
    o_ref[0] = y.T.astype(BF16)


def _gla_kernel(gq_ref, gk_ref, gv_ref, gr_ref, la_ref, tri_ref, st0_ref, gnw_ref, smask_ref,
                o_ref, st_ref):
    @pl.when(pl.program_id(1) == 0)
    def _():
        st_ref[...] = st0_ref[...]

    tri = tri_ref[...]
    la = la_ref[0]
    b = jnp.concatenate([_cumsum_rows(tri, la[g:g + CUMSUM_ROWS]) for g in range(0, GLA_TILE, CUMSUM_ROWS)],
                        axis=0)
    ci = lax.broadcasted_iota(jnp.int32, (GLA_HEADS * CHUNK, CHUNK), 0) % CHUNK
    si = lax.broadcasted_iota(jnp.int32, (GLA_HEADS * CHUNK, CHUNK), 1)
    causal = si <= ci
    smask = smask_ref[...]
    gnw = gnw_ref[...]

    for c in range(GLA_TILE // CHUNK):
        sl = slice(c * CHUNK, (c + 1) * CHUNK)
        bc = b[sl]
        bl = bc[CHUNK - 1:CHUNK]
        q = gq_ref[0, sl, :].astype(F32)
        k = gk_ref[0, sl, :].astype(F32)
        v = gv_ref[0, sl, :]
        qd = (q * GLA_Q_SCALE * jnp.exp(bc)).astype(BF16)
        ki = (k * jnp.exp(-bc)).astype(BF16)
        kd = (k * jnp.exp(bl - bc)).astype(BF16)
        qm = jnp.concatenate([_head_mask(qd, hh, GLA_K_DIM) for hh in range(GLA_HEADS)], axis=0)
        a = jnp.where(causal, _dot_nt(qm, ki), 0.0).astype(BF16)
        st = st_ref[...]
        o_inter = _dot_nt(qd, st.astype(BF16))
        r = gr_ref[0, sl, :].astype(F32)
        for hh in range(GLA_HEADS):
            vs = slice(hh * GLA_V_DIM, (hh + 1) * GLA_V_DIM)
            o = _dot(a[hh * CHUNK:(hh + 1) * CHUNK], v[:, vs]) + o_inter[:, vs]
            o_ref[0, sl, vs] = (_rms(o, gnw) * _silu(r[:, vs])).astype(BF16)
        st_ref[...] = jnp.exp(bl) * st + smask * _dot_tn(v, kd)


def _mlp_kernel(x_ref, oda_ref, ogla_ref, wout_ref, n2_ref, wup_ref, cw_ref, cb_ref, wdown_ref,
                fnw_ref, ut_ref, o_ref, carry_ref, ubuf_ref, act_ref):
    @pl.when(pl.program_id(1) == 0)
    def _():
        carry_ref[...] = ut_ref[...]

    t = x_ref.shape[1]
    mix = jnp.concatenate([oda_ref[0], ogla_ref[0]], axis=1)
    h1 = x_ref[0] + _dot(mix, wout_ref[...])
    xn = _rms(h1, n2_ref[...]).astype(BF16)

    def stage(col0, slot):
        cols = slice(col0, col0 + FF_CHUNK)
        u = _dot(xn, wup_ref[:, cols])
        ubuf_ref[slot, 0:8, :] = carry_ref[:, cols]
        ubuf_ref[slot, 8:8 + t, :] = u
        carry_ref[:, cols] = u[t - 8:t]

    def conv(col0, slot):
        cols = slice(col0, col0 + FF_CHUNK)
        w = cw_ref[:, cols]
        return (w[0:1] * ubuf_ref[slot, 6:6 + t, :] + w[1:2] * ubuf_ref[slot, 7:7 + t, :]
                + w[2:3] * ubuf_ref[slot, 8:8 + t, :] + cb_ref[:, cols])

    def stage_chunk(j):
        stage(j * FF_CHUNK, (2 * j) % CONV_SLOTS)
        stage(D_FF + j * FF_CHUNK, (2 * j + 1) % CONV_SLOTS)

    acc = h1
    n_chunks = D_FF // FF_CHUNK
    stage_chunk(0)
    for j in range(n_chunks):
        if j + 1 < n_chunks:
            stage_chunk(j + 1)
        val = conv(j * FF_CHUNK, (2 * j) % CONV_SLOTS)
        gate = conv(D_FF + j * FF_CHUNK, (2 * j + 1) % CONV_SLOTS)
        act_ref[:, j * FF_CHUNK:(j + 1) * FF_CHUNK] = (val * _silu(gate)).astype(BF16)
        if j % DOWN_GROUP == DOWN_GROUP - 1 or j == n_chunks - 1:
            lo = (j // DOWN_GROUP) * DOWN_GROUP * FF_CHUNK
            hi = (j + 1) * FF_CHUNK
            acc = acc + _dot(act_ref[:, lo:hi], wdown_ref[lo:hi, :])
    o_ref[0] = _rms(acc, fnw_ref[...])


def _bf16_pieces(c):
    out = []
    for _ in range(3):
        p = np.float32(c).astype(BF16).astype(np.float32)
        out.append(p)
        c = np.float32(c) - p
    return out


@functools.lru_cache(maxsize=None)
def _attn_constants():
    assert KEY_TILE == DIAG_TILE == 512 and Q_SUPER <= 1024
    slopes = 2.0 ** (-8.0 * np.arange(1, DA_HEADS + 1, dtype=np.float64) / DA_HEADS)
    c0 = (slopes * LOG2E).astype(np.float32)
    augk = np.zeros((DA_HEADS, KEY_TILE, LANE), np.float32)
    augq = np.zeros((DA_HEADS, LANE, Q_SUPER), np.float32)
    j = np.arange(KEY_TILE)
    iq = np.arange(Q_SUPER)
    for h in range(DA_HEADS):
        a = _bf16_pieces(c0[h])
        for n in range(3):
            augk[h, :, n] = -a[n]
            augq[h, n, :] = iq % 256
            augk[h, :, 3 + n] = -a[n]
            augq[h, 3 + n, :] = (iq // 256) * 256
            augk[h, :, 6 + n] = j % 256
            augq[h, 6 + n, :] = a[n]
            augk[h, :, 9 + n] = (j // 256) * 256
            augq[h, 9 + n, :] = a[n]
    jj, ii = np.meshgrid(np.arange(KEY_TILE), np.arange(DIAG_TILE), indexing="ij")
    visible = (jj // CHUNK) <= (ii // CHUNK)
    bdiag = np.zeros((DA_HEADS, KEY_TILE, 2 * DIAG_TILE), np.float32)
    for h in range(DA_HEADS):
        corr = np.where(jj > ii, -2.0 * c0[h] * (jj - ii), 0.0)
        tile = np.where(visible, corr, -np.inf).astype(np.float32)
        bdiag[h] = np.concatenate([tile, tile], axis=1)
    return c0, augk, augq, bdiag


@functools.lru_cache(maxsize=None)
def _gla_constants():
    r = np.arange(CUMSUM_ROWS)
    tri = ((r[:, None] // CHUNK == r[None, :] // CHUNK) & (r[None, :] <= r[:, None])).astype(np.float32)
    rows = np.arange(GLA_WIDTH)[:, None] // GLA_V_DIM
    cols = np.arange(GLA_KW)[None, :] // GLA_K_DIM
    smask = (rows == cols).astype(np.float32)
    return tri, smask


def _full(shape):
    return pl.BlockSpec(shape, lambda *_: (0,) * len(shape))


def _resident(shape):
    return pl.BlockSpec(shape, lambda *_: (0,) * len(shape), pipeline_mode=pl.Buffered(1))


def _params(sem, flags=None):
    return pltpu.CompilerParams(dimension_semantics=sem, vmem_limit_bytes=VMEM_LIMIT, flags=flags)


def kernel(x, meta_tokens, norm1_w, w_in, lambda_q1, lambda_k1, lambda_q2, lambda_k2, da_subln_w,
           gla_gate_w, gla_gate_b, gla_norm_w, w_out, norm2_w, w_up, conv_w, conv_b, w_down,
           final_norm_w):
    bsz, seq, _ = x.shape
    assert seq % Q_SUPER == 0 and seq % TOK_TILE == 0 and seq % GLA_TILE == 0 and seq % MLP_TILE == 0
    assert norm1_w.shape[0] == 1
    nq = seq // Q_SUPER

    w = w_in[0]
    wq, wk, wv = w[:, 0:512], w[:, 512:1024], w[:, 1024:1536]
    wrest, wlr = w[:, 1536:3072], w[:, 3072:3088]
    wnat = jnp.concatenate([wk, wrest, jnp.pad(wlr, ((0, 0), (0, LANE - GLA_GATE_RANK)))], axis=1).astype(BF16)
    wt = jnp.concatenate([wq, wv], axis=1).T.astype(BF16)
    gw = jnp.pad(gla_gate_w[0], ((0, LANE - GLA_GATE_RANK), (0, 0))).astype(BF16)
    gb = gla_gate_b[0][None].astype(F32)
    n1 = norm1_w[0][None]
    n2 = norm2_w[0][None]
    fnw = final_norm_w[None]
    gnw = gla_norm_w[0][None]
    subw_row = da_subln_w[0][None]
    subw_col = jnp.broadcast_to(da_subln_w[0][:, None], (DA_V_DIM, Q_SUPER))
    lam4 = jnp.concatenate([lambda_q1, lambda_k1, lambda_q2, lambda_k2], axis=0)
    wout = w_out[0].astype(BF16)
    wup = w_up[0].astype(BF16)
    wdown = w_down[0].astype(BF16)
    cw = conv_w[0]
    cb = conv_b[0][None]

    c0_np, augk_np, augq_np, bdiag_np = _attn_constants()
    tri_np, smask_np = _gla_constants()
    c0 = jnp.asarray(c0_np)
    augk = jnp.asarray(augk_np, BF16)
    augq = jnp.asarray(augq_np, BF16)
    bdiag = jnp.asarray(bdiag_np)
    tri = jnp.asarray(tri_np, BF16)
    smask = jnp.asarray(smask_np)

    smem = pl.BlockSpec(memory_space=pltpu.SMEM)
    vmem = pl.BlockSpec(memory_space=pltpu.VMEM)

    k_meta, v_meta, st0, utail = pl.pallas_call(
        _meta_kernel,
        out_shape=(jax.ShapeDtypeStruct((N_META, 512), BF16),
                   jax.ShapeDtypeStruct((N_META, 512), BF16),
                   jax.ShapeDtypeStruct((GLA_WIDTH, GLA_KW), F32),
                   jax.ShapeDtypeStruct((8, 2 * D_FF), F32)),
        in_specs=[vmem, vmem, vmem, vmem, vmem, vmem, vmem, smem, vmem, vmem, vmem, vmem, vmem, vmem],
        out_specs=(vmem, vmem, vmem, vmem),
        compiler_params=pltpu.CompilerParams(vmem_limit_bytes=VMEM_LIMIT),
        name="meta_mixer",
    )(meta_tokens, n1, wnat, wt, gw, gb, lam4, c0, subw_row, gnw, smask, wout, n2, wup)
    kmeta = k_meta.reshape(N_META, DA_HEADS, LANE).transpose(1, 0, 2)
    vtmeta = jnp.pad(v_meta.reshape(N_META, DA_HEADS, DA_V_DIM).transpose(1, 2, 0),
                     ((0, 0), (0, 0), (0, LANE - N_META)))

    nt = seq // TOK_TILE
    tok = lambda width: pl.BlockSpec((1, TOK_TILE, width), lambda b, t: (b, t, 0))
    tokt = pl.BlockSpec((1, 512, TOK_TILE), lambda b, t: (b, 0, t))
    kcat, qt, vt, gq, gk, gv, gr, la = pl.pallas_call(
        _proj_kernel,
        grid=(bsz, nt),
        out_shape=(jax.ShapeDtypeStruct((bsz, seq, 512), BF16),
                   jax.ShapeDtypeStruct((bsz, 512, seq), BF16),
                   jax.ShapeDtypeStruct((bsz, 512, seq), BF16),
                   jax.ShapeDtypeStruct((bsz, seq, GLA_KW), BF16),
                   jax.ShapeDtypeStruct((bsz, seq, GLA_KW), BF16),
                   jax.ShapeDtypeStruct((bsz, seq, GLA_WIDTH), BF16),
                   jax.ShapeDtypeStruct((bsz, seq, GLA_WIDTH), BF16),
                   jax.ShapeDtypeStruct((bsz, seq, GLA_KW), F32)),
        in_specs=[tok(D_MODEL), _full((1, D_MODEL)), _resident((D_MODEL, NAT_COLS)),
                  _resident((1024, D_MODEL)), _full((LANE, GLA_KW)), _full((1, GLA_KW))],
        out_specs=(tok(512), tokt, tokt, tok(GLA_KW), tok(GLA_KW), tok(GLA_WIDTH), tok(GLA_WIDTH),
                   tok(GLA_KW)),
        compiler_params=_params(("parallel", "arbitrary")),
        name="in_proj",
    )(x, n1, wnat, wt, gw, gb)

    per_head = lambda shape: pl.BlockSpec((1,) + shape, lambda b, h, q: (h, 0, 0))
    o_da = pl.pallas_call(
        _attn_kernel,
        grid=(bsz, DA_HEADS, nq),
        out_shape=jax.ShapeDtypeStruct((bsz, seq, DA_WIDTH), BF16),
        in_specs=[smem, _full((4, DA_QK_DIM)),
                  pl.BlockSpec((1, 2 * DA_QK_DIM, Q_SUPER), lambda b, h, q: (b, h, q)),
                  pl.BlockSpec((1, seq, LANE), lambda b, h, q: (b, 0, h)),
                  pl.BlockSpec((1, DA_V_DIM, seq), lambda b, h, q: (b, h, 0)),
                  per_head((KEY_TILE, LANE)), per_head((LANE, Q_SUPER)), per_head((KEY_TILE, 2 * DIAG_TILE)),
                  per_head((N_META, LANE)), per_head((DA_V_DIM, LANE)), _full((DA_V_DIM, Q_SUPER))],
        out_specs=pl.BlockSpec((1, Q_SUPER, DA_V_DIM), lambda b, h, q: (b, q, h)),
        scratch_shapes=[pltpu.VMEM((2 * LANE, 2 * Q_SUPER), BF16),
                        pltpu.VMEM((1, 2 * Q_SUPER), F32),
                        pltpu.VMEM((1, 2 * Q_SUPER), F32),
                        pltpu.VMEM((DA_V_DIM, 2 * Q_SUPER), F32),
                        pltpu.VMEM((KEY_TILE, 2 * Q_SUPER), F32),
                        pltpu.VMEM((KEY_TILE, 2 * Q_SUPER), F32),
                        pltpu.VMEM((1, 2 * Q_SUPER), F32),
                        pltpu.VMEM((1, 2 * Q_SUPER), F32)],
        compiler_params=_params(("parallel", "parallel", "arbitrary")),
        name="diff_attn",
    )(c0, lam4, qt, kcat, vt, augk, augq, bdiag, kmeta, vtmeta, subw_col)

    ng = seq // GLA_TILE
    gtok = lambda width: pl.BlockSpec((1, GLA_TILE, width), lambda b, t: (b, t, 0))
    o_gla = pl.pallas_call(
        _gla_kernel,
        grid=(bsz, ng),
        out_shape=jax.ShapeDtypeStruct((bsz, seq, GLA_WIDTH), BF16),
        in_specs=[gtok(GLA_KW), gtok(GLA_KW), gtok(GLA_WIDTH), gtok(GLA_WIDTH), gtok(GLA_KW),
                  _full((CUMSUM_ROWS, CUMSUM_ROWS)), _full((GLA_WIDTH, GLA_KW)), _full((1, GLA_V_DIM)),
                  _full((GLA_WIDTH, GLA_KW))],
        out_specs=gtok(GLA_WIDTH),
        scratch_shapes=[pltpu.VMEM((GLA_WIDTH, GLA_KW), F32)],
        compiler_params=_params(("parallel", "arbitrary")),
        name="gla",
    )(gq, gk, gv, gr, la, tri, st0, gnw, smask)

    mtok = lambda width: pl.BlockSpec((1, MLP_TILE, width), lambda b, t: (b, t, 0))
    out = pl.pallas_call(
        _mlp_kernel,
        grid=(bsz, seq // MLP_TILE),
        out_shape=jax.ShapeDtypeStruct((bsz, seq, D_MODEL), F32),
        in_specs=[mtok(D_MODEL), mtok(DA_WIDTH), mtok(GLA_WIDTH), _resident((D_MODEL, D_MODEL)),
                  _full((1, D_MODEL)), _resident((D_MODEL, 2 * D_FF)), _full((3, 2 * D_FF)),
                  _full((1, 2 * D_FF)), _resident((D_FF, D_MODEL)), _full((1, D_MODEL)),
                  _full((8, 2 * D_FF))],
        out_specs=mtok(D_MODEL),
        scratch_shapes=[pltpu.VMEM((8, 2 * D_FF), F32),
                        pltpu.VMEM((CONV_SLOTS, MLP_TILE + 8, FF_CHUNK), F32),
                        pltpu.VMEM((MLP_TILE, D_FF), BF16)],
        compiler_params=_params(("parallel", "arbitrary")),
        name="out_proj_mlp",
    )(x, o_da, o_gla, wout, n2, wup, cw, cb, wdown, fnw, utail)
    return out
```

```python
import functools
import math

import numpy as np
import jax
import jax.numpy as jnp
from jax import lax
from jax.experimental import pallas as pl
from jax.experimental.pallas import tpu as pltpu

F32 = jnp.float32
BF16 = jnp.bfloat16

D_MODEL = 1024
N_META = 16
CHUNK = 64
Q_SUPER = 1024
DIAG_TILE = 512
DA_HEADS = 4
DA_QK_DIM = 64
DA_V_DIM = 128
DA_WIDTH = DA_HEADS * DA_V_DIM
GLA_HEADS = 4
GLA_K_DIM = 64
GLA_V_DIM = 128
GLA_KW = GLA_HEADS * GLA_K_DIM
GLA_WIDTH = GLA_HEADS * GLA_V_DIM
GLA_GATE_RANK = 16
GLA_GATE_TAU = 16.0
D_FF = 2816
RMS_EPS = 1e-6
LAM_INIT = 0.8 - 0.6 * math.exp(-0.3 * 0)

LOG2E = 1.4426950408889634
Q_SCALE = DA_QK_DIM ** -0.5 * LOG2E
GLA_Q_SCALE = GLA_K_DIM ** -0.5
LANE = 128
KEY_TILE = 512
SCORE_STRIP = 256
GLA_TILE = 512
CUMSUM_ROWS = 128
TOK_TILE = 512
MLP_TILE = 512
FF_CHUNK = 256
CONV_SLOTS = 4
DOWN_GROUP = 11
NAT_COLS = 2048 + LANE
VMEM_LIMIT = 56 * 1024 * 1024

_NT = (((1,), (1,)), ((), ()))
_TN = (((0,), (0,)), ((), ()))


def _dot(a, b):
    return jnp.dot(a, b, preferred_element_type=F32)


def _dot_nt(a, b):
    return lax.dot_general(a, b, _NT, preferred_element_type=F32)


def _dot_tn(a, b):
    return lax.dot_general(a, b, _TN, preferred_element_type=F32)


def _rms(x, w):
    return x * lax.rsqrt(jnp.mean(x * x, axis=-1, keepdims=True) + RMS_EPS) * w


def _log_sigmoid(z):
    return jnp.minimum(z, 0.0) - jnp.log(1.0 + jnp.exp(-jnp.abs(z)))


def _silu(g):
    hg = 0.5 * g
    return hg * jnp.tanh(hg) + hg


def _split3(x):
    hi = x.astype(BF16)
    r = x - hi.astype(F32)
    mid = r.astype(BF16)
    lo = (r - mid.astype(F32)).astype(BF16)
    return hi, mid, lo


def _cumsum_rows(tri, x):
    hi, mid, lo = _split3(x)
    return _dot(tri, hi) + _dot(tri, mid) + _dot(tri, lo)


def _lam(lam4):
    s1 = jnp.sum(lam4[0:1] * lam4[1:2], axis=-1, keepdims=True)
    s2 = jnp.sum(lam4[2:3] * lam4[3:4], axis=-1, keepdims=True)
    return jnp.exp(s1) - jnp.exp(s2) + LAM_INIT


def _head_mask(x, h, width):
    lane = lax.broadcasted_iota(jnp.int32, x.shape, 1)
    return jnp.where((lane >= h * width) & (lane < (h + 1) * width), x, jnp.zeros_like(x))


def _meta_kernel(meta_ref, n1_ref, wnat_ref, wt_ref, gw_ref, gb_ref, lam4_ref, c0_ref, subw_ref,
                 gnw_ref, smask_ref, wout_ref, n2_ref, wup_ref,
                 k_out, v_out, st_out, ut_out):
    m = N_META
    x = meta_ref[...]
    ub = _rms(x, n1_ref[...]).astype(BF16)
    nat = _dot(ub, wnat_ref[...])
    qv = _dot_nt(ub, wt_ref[...])
    kb = nat[:, 0:512].astype(BF16)
    qb = (qv[:, 0:512] * Q_SCALE).astype(BF16)
    vb = qv[:, 512:1024].astype(BF16)
    k_out[...] = kb
    v_out[...] = vb

    lam = _lam(lam4_ref[...])
    ii = lax.broadcasted_iota(jnp.int32, (m, m), 0)
    jj = lax.broadcasted_iota(jnp.int32, (m, m), 1)
    dist = jnp.abs(ii - jj).astype(F32)

    def softmax2(s):
        p = jnp.exp2(s - jnp.max(s, axis=-1, keepdims=True))
        return p / jnp.sum(p, axis=-1, keepdims=True)

    o_da = []
    for h in range(DA_HEADS):
        bias = -c0_ref[h] * dist
        p1 = softmax2(_dot_nt(_head_mask(qb, 2 * h, DA_QK_DIM), kb) + bias)
        p2 = softmax2(_dot_nt(_head_mask(qb, 2 * h + 1, DA_QK_DIM), kb) + bias)
        o = _dot((p1 - lam * p2).astype(BF16), vb[:, h * DA_V_DIM:(h + 1) * DA_V_DIM])
        o_da.append(_rms(o, subw_ref[...]) * (1.0 - LAM_INIT))
    o_da = jnp.concatenate(o_da, axis=1)

    gq = nat[:, 512:768].astype(BF16).astype(F32)
    gk = nat[:, 768:1024].astype(BF16).astype(F32)
    gv = nat[:, 1024:1536].astype(BF16)
    gr = nat[:, 1536:2048].astype(BF16).astype(F32)
    g16 = nat[:, 2048:NAT_COLS].astype(BF16)
    la = _log_sigmoid(_dot(g16, gw_ref[...]) + gb_ref[...]) / GLA_GATE_TAU
    tri = (jj <= ii).astype(BF16)
    b = _cumsum_rows(tri, la)
    bl = b[m - 1:m]
    qd = (gq * GLA_Q_SCALE * jnp.exp(b)).astype(BF16)
    ki = (gk * jnp.exp(-b)).astype(BF16)
    kd = (gk * jnp.exp(bl - b)).astype(BF16)
    o_gla = []
    for h in range(GLA_HEADS):
        a = _dot_nt(_head_mask(qd, h, GLA_K_DIM), ki)
        a = jnp.where(jj <= ii, a, 0.0).astype(BF16)
        o = _dot(a, gv[:, h * GLA_V_DIM:(h + 1) * GLA_V_DIM])
        o_gla.append(_rms(o, gnw_ref[...]))
    o_gla = jnp.concatenate(o_gla, axis=1) * _silu(gr)
    st_out[...] = smask_ref[...] * _dot_tn(gv, kd)

    mix = jnp.concatenate([o_da, o_gla], axis=1).astype(BF16)
    h1 = x + _dot(mix, wout_ref[...])
    u = _dot(_rms(h1, n2_ref[...]).astype(BF16), wup_ref[...])
    ut_out[...] = u[m - 8:m]


def _proj_kernel(x_ref, n1_ref, wnat_ref, wt_ref, gw_ref, gb_ref,
                 k_out, qt_out, vt_out, gq_out, gk_out, gv_out, gr_out, la_out):
    ub = _rms(x_ref[0], n1_ref[...]).astype(BF16)
    nat = _dot(ub, wnat_ref[...])
    k_out[0] = nat[:, 0:512].astype(BF16)
    gq_out[0] = nat[:, 512:768].astype(BF16)
    gk_out[0] = nat[:, 768:1024].astype(BF16)
    gv_out[0] = nat[:, 1024:1536].astype(BF16)
    gr_out[0] = nat[:, 1536:2048].astype(BF16)
    g16 = nat[:, 2048:NAT_COLS].astype(BF16)
    la_out[0] = _log_sigmoid(_dot(g16, gw_ref[...]) + gb_ref[...]) / GLA_GATE_TAU
    tt = _dot_nt(wt_ref[...], ub)
    qt_out[0] = (tt[0:512] * Q_SCALE).astype(BF16)
    vt_out[0] = tt[512:1024].astype(BF16)


def _attn_kernel(c0_ref, lam4_ref, qt_ref, k_ref, vt_ref, augk_ref, augq_ref, bdiag_ref,
                 kmeta_ref, vtmeta_ref, subw_ref, o_ref,
                 qrhs_ref, m_ref, l_ref, acc_ref, sa_ref, sb_ref, mxa_ref, mxb_ref):
    h = pl.program_id(1)
    qs = pl.program_id(2)
    c0 = c0_ref[h]
    q0 = qs * Q_SUPER

    qt = qt_ref[0]
    zeros = jnp.zeros((DA_QK_DIM, Q_SUPER), BF16)
    qrhs_ref[0:64, 0:Q_SUPER] = qt[0:64]
    qrhs_ref[64:128, 0:Q_SUPER] = zeros
    qrhs_ref[0:64, Q_SUPER:2 * Q_SUPER] = zeros
    qrhs_ref[64:128, Q_SUPER:2 * Q_SUPER] = qt[64:128]
    qrhs_ref[128:256, 0:Q_SUPER] = augq_ref[0]
    qrhs_ref[128:256, Q_SUPER:2 * Q_SUPER] = augq_ref[0]

    m_ref[...] = jnp.full(m_ref.shape, -jnp.inf, F32)
    l_ref[...] = jnp.zeros(l_ref.shape, F32)
    acc_ref[...] = jnp.zeros(acc_ref.shape, F32)

    def softmax_pv(s, mx, vt, c, lanes):
        m_old = m_ref[:, lanes]
        m_new = jnp.maximum(m_old, mx + c)
        alpha = jnp.exp2(m_old - m_new)
        p = jnp.exp2(s - (m_new - c))
        l_ref[:, lanes] = alpha * l_ref[:, lanes] + jnp.sum(p, axis=0, keepdims=True)
        pb = p.astype(BF16)
        if pb.shape[0] < vt.shape[1]:
            pb = jnp.concatenate([pb, jnp.zeros((vt.shape[1] - pb.shape[0], pb.shape[1]), BF16)], axis=0)
        acc_ref[:, lanes] = alpha * acc_ref[:, lanes] + _dot(vt, pb)
        m_ref[:, lanes] = m_new

    n_strips = 2 * Q_SUPER // SCORE_STRIP
    per_map = Q_SUPER // SCORE_STRIP
    strips = [slice(n * SCORE_STRIP, (n + 1) * SCORE_STRIP) for n in range(n_strips)]

    def query_block(n):
        return (n % per_map) * SCORE_STRIP // DIAG_TILE

    def key_lhs(j):
        k0 = pl.multiple_of(j * KEY_TILE, KEY_TILE)
        return jnp.concatenate([k_ref[0, pl.ds(k0, KEY_TILE), :], augk_ref[0]], axis=1)

    def score_strip(lhs, s_ref, mx_ref, n):
        s = _dot(lhs, qrhs_ref[:, strips[n]])
        s_ref[:, strips[n]] = s
        mx_ref[:, strips[n]] = jnp.max(s, axis=0, keepdims=True)

    def tile_step(j, s_ref, mx_ref, diag_block, nxt=None):
        k0 = pl.multiple_of(j * KEY_TILE, KEY_TILE)
        vt = vt_ref[0, :, pl.ds(k0, KEY_TILE)]
        c = -c0 * (q0 - k0).astype(F32)
        todo = []
        if nxt is not None:
            jn, sn_ref, mxn_ref, todo = nxt
            todo = list(todo)
            lhs_n = key_lhs(jn)
        for n, lanes in enumerate(strips):
            if todo:
                score_strip(lhs_n, sn_ref, mxn_ref, todo.pop(0))
            if diag_block is None or query_block(n) > diag_block:
                softmax_pv(s_ref[:, lanes], mx_ref[:, lanes], vt, c, lanes)
            elif query_block(n) == diag_block:
                sub = (n % per_map) * SCORE_STRIP % DIAG_TILE
                rows = KEY_TILE // 2 if sub == 0 else KEY_TILE
                b0 = (n // per_map) * DIAG_TILE + sub
                s = s_ref[0:rows, lanes] + bdiag_ref[0, 0:rows, b0:b0 + SCORE_STRIP]
                softmax_pv(s, jnp.max(s, axis=0, keepdims=True), vt[:, 0:rows], c, lanes)
        assert not todo

    every = list(range(n_strips))

    s_meta = _dot(jnp.concatenate([kmeta_ref[0], augk_ref[0, 0:N_META, :]], axis=1), qrhs_ref[...])
    lhs0 = key_lhs(0)
    for n, lanes in enumerate(strips):
        score_strip(lhs0, sa_ref, mxa_ref, n)
        sm = s_meta[:, lanes]
        softmax_pv(sm, jnp.max(sm, axis=0, keepdims=True), vtmeta_ref[0],
                   -c0 * (q0 + N_META).astype(F32), lanes)

    n_diag = Q_SUPER // KEY_TILE
    assert n_diag == 2 and KEY_TILE == DIAG_TILE
    n_past = qs * n_diag

    def pair_body(jj, carry):
        j = 2 * jj
        tile_step(j, sa_ref, mxa_ref, None, (j + 1, sb_ref, mxb_ref, every))
        tile_step(j + 1, sb_ref, mxb_ref, None, (j + 2, sa_ref, mxa_ref, every))
        return carry

    lax.fori_loop(0, qs, pair_body, 0)

    later = [n for n in every if query_block(n) == 1]
    tile_step(n_past, sa_ref, mxa_ref, 0, (n_past + 1, sb_ref, mxb_ref, later))
    tile_step(n_past + 1, sb_ref, mxb_ref, 1)

    inv = 1.0 / l_ref[...]
    acc = acc_ref[...]
    o = (acc[:, 0:Q_SUPER] * inv[:, 0:Q_SUPER]
         - _lam(lam4_ref[...]) * (acc[:, Q_SUPER:2 * Q_SUPER] * inv[:, Q_SUPER:2 * Q_SUPER]))
    y = o * lax.rsqrt(jnp.mean(o * o, axis=0, keepdims=True) + RMS_EPS) * subw_ref[...] * (1.0 - LAM_INIT)
    o_ref[0] = y.T.astype(BF16)


def _gla_kernel(gq_ref, gk_ref, gv_ref, gr_ref, la_ref, tri_ref, st0_ref, gnw_ref, smask_ref,
                o_ref, st_ref):
    @pl.when(pl.program_id(1) == 0)
    def _():
        st_ref[...] = st0_ref[...]

    tri = tri_ref[...]
    la = la_ref[0]
    b = jnp.concatenate([_cumsum_rows(tri, la[g:g + CUMSUM_ROWS]) for g in range(0, GLA_TILE, CUMSUM_ROWS)],
                        axis=0)
    ci = lax.broadcasted_iota(jnp.int32, (GLA_HEADS * CHUNK, CHUNK), 0) % CHUNK
    si = lax.broadcasted_iota(jnp.int32, (GLA_HEADS * CHUNK, CHUNK), 1)
    causal = si <= ci
    smask = smask_ref[...]
    gnw = gnw_ref[...]

    for c in range(GLA_TILE // CHUNK):
        sl = slice(c * CHUNK, (c + 1) * CHUNK)
        bc = b[sl]
        bl = bc[CHUNK - 1:CHUNK]
        q = gq_ref[0, sl, :].astype(F32)
        k = gk_ref[0, sl, :].astype(F32)
        v = gv_ref[0, sl, :]
        qd = (q * GLA_Q_SCALE * jnp.exp(bc)).astype(BF16)
        ki = (k * jnp.exp(-bc)).astype(BF16)
        kd = (k * jnp.exp(bl - bc)).astype(BF16)
        qm = jnp.concatenate([_head_mask(qd, hh, GLA_K_DIM) for hh in range(GLA_HEADS)], axis=0)
        a = jnp.where(causal, _dot_nt(qm, ki), 0.0).astype(BF16)
        st = st_ref[...]
        o_inter = _dot_nt(qd, st.astype(BF16))
        r = gr_ref[0, sl, :].astype(F32)
        for hh in range(GLA_HEADS):
            vs = slice(hh * GLA_V_DIM, (hh + 1) * GLA_V_DIM)
            o = _dot(a[hh * CHUNK:(hh + 1) * CHUNK], v[:, vs]) + o_inter[:, vs]
            o_ref[0, sl, vs] = (_rms(o, gnw) * _silu(r[:, vs])).astype(BF16)
        st_ref[...] = jnp.exp(bl) * st + smask * _dot_tn(v, kd)


def _mlp_kernel(x_ref, oda_ref, ogla_ref, wout_ref, n2_ref, wup_ref, cw_ref, cb_ref, wdown_ref,
                fnw_ref, ut_ref, o_ref, carry_ref, ubuf_ref, act_ref):
    @pl.when(pl.program_id(1) == 0)
    def _():
        carry_ref[...] = ut_ref[...]

    t = x_ref.shape[1]
    mix = jnp.concatenate([oda_ref[0], ogla_ref[0]], axis=1)
    h1 = x_ref[0] + _dot(mix, wout_ref[...])
    xn = _rms(h1, n2_ref[...]).astype(BF16)

    def stage(col0, slot):
        cols = slice(col0, col0 + FF_CHUNK)
        u = _dot(xn, wup_ref[:, cols])
        ubuf_ref[slot, 0:8, :] = carry_ref[:, cols]
        ubuf_ref[slot, 8:8 + t, :] = u
        carry_ref[:, cols] = u[t - 8:t]

    def conv(col0, slot):
        cols = slice(col0, col0 + FF_CHUNK)
        w = cw_ref[:, cols]
        return (w[0:1] * ubuf_ref[slot, 6:6 + t, :] + w[1:2] * ubuf_ref[slot, 7:7 + t, :]
                + w[2:3] * ubuf_ref[slot, 8:8 + t, :] + cb_ref[:, cols])

    def stage_chunk(j):
        stage(j * FF_CHUNK, (2 * j) % CONV_SLOTS)
        stage(D_FF + j * FF_CHUNK, (2 * j + 1) % CONV_SLOTS)

    acc = h1
    n_chunks = D_FF // FF_CHUNK
    stage_chunk(0)
    for j in range(n_chunks):
        if j + 1 < n_chunks:
            stage_chunk(j + 1)
        val = conv(j * FF_CHUNK, (2 * j) % CONV_SLOTS)
        gate = conv(D_FF + j * FF_CHUNK, (2 * j + 1) % CONV_SLOTS)
        act_ref[:, j * FF_CHUNK:(j + 1) * FF_CHUNK] = (val * _silu(gate)).astype(BF16)
        if j % DOWN_GROUP == DOWN_GROUP - 1 or j == n_chunks - 1:
            lo = (j // DOWN_GROUP) * DOWN_GROUP * FF_CHUNK
            hi = (j + 1) * FF_CHUNK
            acc = acc + _dot(act_ref[:, lo:hi], wdown_ref[lo:hi, :])
    o_ref[0] = _rms(acc, fnw_ref[...])


def _bf16_pieces(c):
    out = []
    for _ in range(3):
        p = np.float32(c).astype(BF16).astype(np.float32)
        out.append(p)
        c = np.float32(c) - p
    return out


@functools.lru_cache(maxsize=None)
def _attn_constants():
    assert KEY_TILE == DIAG_TILE == 512 and Q_SUPER <= 1024
    slopes = 2.0 ** (-8.0 * np.arange(1, DA_HEADS + 1, dtype=np.float64) / DA_HEADS)
    c0 = (slopes * LOG2E).astype(np.float32)
    augk = np.zeros((DA_HEADS, KEY_TILE, LANE), np.float32)
    augq = np.zeros((DA_HEADS, LANE, Q_SUPER), np.float32)
    j = np.arange(KEY_TILE)
    iq = np.arange(Q_SUPER)
    for h in range(DA_HEADS):
        a = _bf16_pieces(c0[h])
        for n in range(3):
            augk[h, :, n] = -a[n]
            augq[h, n, :] = iq % 256
            augk[h, :, 3 + n] = -a[n]
            augq[h, 3 + n, :] = (iq // 256) * 256
            augk[h, :, 6 + n] = j % 256
            augq[h, 6 + n, :] = a[n]
            augk[h, :, 9 + n] = (j // 256) * 256
            augq[h, 9 + n, :] = a[n]
    jj, ii = np.meshgrid(np.arange(KEY_TILE), np.arange(DIAG_TILE), indexing="ij")
    visible = (jj // CHUNK) <= (ii // CHUNK)
    bdiag = np.zeros((DA_HEADS, KEY_TILE, 2 * DIAG_TILE), np.float32)
    for h in range(DA_HEADS):
        corr = np.where(jj > ii, -2.0 * c0[h] * (jj - ii), 0.0)
        tile = np.where(visible, corr, -np.inf).astype(np.float32)
        bdiag[h] = np.concatenate([tile, tile], axis=1)
    return c0, augk, augq, bdiag


@functools.lru_cache(maxsize=None)
def _gla_constants():
    r = np.arange(CUMSUM_ROWS)
    tri = ((r[:, None] // CHUNK == r[None, :] // CHUNK) & (r[None, :] <= r[:, None])).astype(np.float32)
    rows = np.arange(GLA_WIDTH)[:, None] // GLA_V_DIM
    cols = np.arange(GLA_KW)[None, :] // GLA_K_DIM
    smask = (rows == cols).astype(np.float32)
    return tri, smask


def _full(shape):
    return pl.BlockSpec(shape, lambda *_: (0,) * len(shape))


def _resident(shape):
    return pl.BlockSpec(shape, lambda *_: (0,) * len(shape), pipeline_mode=pl.Buffered(1))


def _params(sem, flags=None):
    return pltpu.CompilerParams(dimension_semantics=sem, vmem_limit_bytes=VMEM_LIMIT, flags=flags)


def kernel(x, meta_tokens, norm1_w, w_in, lambda_q1, lambda_k1, lambda_q2, lambda_k2, da_subln_w,
           gla_gate_w, gla_gate_b, gla_norm_w, w_out, norm2_w, w_up, conv_w, conv_b, w_down,
           final_norm_w):
    bsz, seq, _ = x.shape
    assert seq % Q_SUPER == 0 and seq % TOK_TILE == 0 and seq % GLA_TILE == 0 and seq % MLP_TILE == 0
    assert norm1_w.shape[0] == 1
    nq = seq // Q_SUPER

    w = w_in[0]
    wq, wk, wv = w[:, 0:512], w[:, 512:1024], w[:, 1024:1536]
    wrest, wlr = w[:, 1536:3072], w[:, 3072:3088]
    wnat = jnp.concatenate([wk, wrest, jnp.pad(wlr, ((0, 0), (0, LANE - GLA_GATE_RANK)))], axis=1).astype(BF16)
    wt = jnp.concatenate([wq, wv], axis=1).T.astype(BF16)
    gw = jnp.pad(gla_gate_w[0], ((0, LANE - GLA_GATE_RANK), (0, 0))).astype(BF16)
    gb = gla_gate_b[0][None].astype(F32)
    n1 = norm1_w[0][None]
    n2 = norm2_w[0][None]
    fnw = final_norm_w[None]
    gnw = gla_norm_w[0][None]
    subw_row = da_subln_w[0][None]
    subw_col = jnp.broadcast_to(da_subln_w[0][:, None], (DA_V_DIM, Q_SUPER))
    lam4 = jnp.concatenate([lambda_q1, lambda_k1, lambda_q2, lambda_k2], axis=0)
    wout = w_out[0].astype(BF16)
    wup = w_up[0].astype(BF16)
    wdown = w_down[0].astype(BF16)
    cw = conv_w[0]
    cb = conv_b[0][None]

    c0_np, augk_np, augq_np, bdiag_np = _attn_constants()
    tri_np, smask_np = _gla_constants()
    c0 = jnp.asarray(c0_np)
    augk = jnp.asarray(augk_np, BF16)
    augq = jnp.asarray(augq_np, BF16)
    bdiag = jnp.asarray(bdiag_np)
    tri = jnp.asarray(tri_np, BF16)
    smask = jnp.asarray(smask_np)

    smem = pl.BlockSpec(memory_space=pltpu.SMEM)
    vmem = pl.BlockSpec(memory_space=pltpu.VMEM)

    k_meta, v_meta, st0, utail = pl.pallas_call(
        _meta_kernel,
        out_shape=(jax.ShapeDtypeStruct((N_META, 512), BF16),
                   jax.ShapeDtypeStruct((N_META, 512), BF16),
                   jax.ShapeDtypeStruct((GLA_WIDTH, GLA_KW), F32),
                   jax.ShapeDtypeStruct((8, 2 * D_FF), F32)),
        in_specs=[vmem, vmem, vmem, vmem, vmem, vmem, vmem, smem, vmem, vmem, vmem, vmem, vmem, vmem],
        out_specs=(vmem, vmem, vmem, vmem),
        compiler_params=pltpu.CompilerParams(vmem_limit_bytes=VMEM_LIMIT),
        name="meta_mixer",
    )(meta_tokens, n1, wnat, wt, gw, gb, lam4, c0, subw_row, gnw, smask, wout, n2, wup)
    kmeta = k_meta.reshape(N_META, DA_HEADS, LANE).transpose(1, 0, 2)
    vtmeta = jnp.pad(v_meta.reshape(N_META, DA_HEADS, DA_V_DIM).transpose(1, 2, 0),
                     ((0, 0), (0, 0), (0, LANE - N_META)))

    nt = seq // TOK_TILE
    tok = lambda width: pl.BlockSpec((1, TOK_TILE, width), lambda b, t: (b, t, 0))
    tokt = pl.BlockSpec((1, 512, TOK_TILE), lambda b, t: (b, 0, t))
    kcat, qt, vt, gq, gk, gv, gr, la = pl.pallas_call(
        _proj_kernel,
        grid=(bsz, nt),
        out_shape=(jax.ShapeDtypeStruct((bsz, seq, 512), BF16),
                   jax.ShapeDtypeStruct((bsz, 512, seq), BF16),
                   jax.ShapeDtypeStruct((bsz, 512, seq), BF16),
                   jax.ShapeDtypeStruct((bsz, seq, GLA_KW), BF16),
                   jax.ShapeDtypeStruct((bsz, seq, GLA_KW), BF16),
                   jax.ShapeDtypeStruct((bsz, seq, GLA_WIDTH), BF16),
                   jax.ShapeDtypeStruct((bsz, seq, GLA_WIDTH), BF16),
                   jax.ShapeDtypeStruct((bsz, seq, GLA_KW), F32)),
        in_specs=[tok(D_MODEL), _full((1, D_MODEL)), _resident((D_MODEL, NAT_COLS)),
                  _resident((1024, D_MODEL)), _full((LANE, GLA_KW)), _full((1, GLA_KW))],
        out_specs=(tok(512), tokt, tokt, tok(GLA_KW), tok(GLA_KW), tok(GLA_WIDTH), tok(GLA_WIDTH),
                   tok(GLA_KW)),
        compiler_params=_params(("parallel", "arbitrary")),
        name="in_proj",
    )(x, n1, wnat, wt, gw, gb)

    per_head = lambda shape: pl.BlockSpec((1,) + shape, lambda b, h, q: (h, 0, 0))
    o_da = pl.pallas_call(
        _attn_kernel,
        grid=(bsz, DA_HEADS, nq),
        out_shape=jax.ShapeDtypeStruct((bsz, seq, DA_WIDTH), BF16),
        in_specs=[smem, _full((4, DA_QK_DIM)),
                  pl.BlockSpec((1, 2 * DA_QK_DIM, Q_SUPER), lambda b, h, q: (b, h, q)),
                  pl.BlockSpec((1, seq, LANE), lambda b, h, q: (b, 0, h)),
                  pl.BlockSpec((1, DA_V_DIM, seq), lambda b, h, q: (b, h, 0)),
                  per_head((KEY_TILE, LANE)), per_head((LANE, Q_SUPER)), per_head((KEY_TILE, 2 * DIAG_TILE)),
                  per_head((N_META, LANE)), per_head((DA_V_DIM, LANE)), _full((DA_V_DIM, Q_SUPER))],
        out_specs=pl.BlockSpec((1, Q_SUPER, DA_V_DIM), lambda b, h, q: (b, q, h)),
        scratch_shapes=[pltpu.VMEM((2 * LANE, 2 * Q_SUPER), BF16),
                        pltpu.VMEM((1, 2 * Q_SUPER), F32),
                        pltpu.VMEM((1, 2 * Q_SUPER), F32),
                        pltpu.VMEM((DA_V_DIM, 2 * Q_SUPER), F32),
                        pltpu.VMEM((KEY_TILE, 2 * Q_SUPER), F32),
                        pltpu.VMEM((KEY_TILE, 2 * Q_SUPER), F32),
                        pltpu.VMEM((1, 2 * Q_SUPER), F32),
                        pltpu.VMEM((1, 2 * Q_SUPER), F32)],
        compiler_params=_params(("parallel", "parallel", "arbitrary")),
        name="diff_attn",
    )(c0, lam4, qt, kcat, vt, augk, augq, bdiag, kmeta, vtmeta, subw_col)

    ng = seq // GLA_TILE
    gtok = lambda width: pl.BlockSpec((1, GLA_TILE, width), lambda b, t: (b, t, 0))
    o_gla = pl.pallas_call(
        _gla_kernel,
        grid=(bsz, ng),
        out_shape=jax.ShapeDtypeStruct((bsz, seq, GLA_WIDTH), BF16),
        in_specs=[gtok(GLA_KW), gtok(GLA_KW), gtok(GLA_WIDTH), gtok(GLA_WIDTH), gtok(GLA_KW),
                  _full((CUMSUM_ROWS, CUMSUM_ROWS)), _full((GLA_WIDTH, GLA_KW)), _full((1, GLA_V_DIM)),
                  _full((GLA_WIDTH, GLA_KW))],
        out_specs=gtok(GLA_WIDTH),
        scratch_shapes=[pltpu.VMEM((GLA_WIDTH, GLA_KW), F32)],
        compiler_params=_params(("parallel", "arbitrary")),
        name="gla",
    )(gq, gk, gv, gr, la, tri, st0, gnw, smask)

    mtok = lambda width: pl.BlockSpec((1, MLP_TILE, width), lambda b, t: (b, t, 0))
    out = pl.pallas_call(
        _mlp_kernel,
        grid=(bsz, seq // MLP_TILE),
        out_shape=jax.ShapeDtypeStruct((bsz, seq, D_MODEL), F32),
        in_specs=[mtok(D_MODEL), mtok(DA_WIDTH), mtok(GLA_WIDTH), _resident((D_MODEL, D_MODEL)),
                  _full((1, D_MODEL)), _resident((D_MODEL, 2 * D_FF)), _full((3, 2 * D_FF)),
                  _full((1, 2 * D_FF)), _resident((D_FF, D_MODEL)), _full((1, D_MODEL)),
                  _full((8, 2 * D_FF))],
        out_specs=mtok(D_MODEL),
        scratch_shapes=[pltpu.VMEM((8, 2 * D_FF), F32),
                        pltpu.VMEM((CONV_SLOTS, MLP_TILE + 8, FF_CHUNK), F32),
                        pltpu.VMEM((MLP_TILE, D_FF), BF16)],
        compiler_params=_params(("parallel", "arbitrary")),
        name="out_proj_mlp",
    )(x, o_da, o_gla, wout, n2, wup, cw, cb, wdown, fnw, utail)
    return out
```

```python
import functools
import math

import numpy as np
import jax
import jax.numpy as jnp
from jax import lax
from jax.experimental import pallas as pl
from jax.experimental.pallas import tpu as pltpu

F32 = jnp.float32
BF16 = jnp.bfloat16

D_MODEL = 1024
N_META = 16
CHUNK = 64
Q_SUPER = 1024
DIAG_TILE = 512
DA_HEADS = 4
DA_QK_DIM = 64
DA_V_DIM = 128
DA_WIDTH = DA_HEADS * DA_V_DIM
GLA_HEADS = 4
GLA_K_DIM = 64
GLA_V_DIM = 128
GLA_KW = GLA_HEADS * GLA_K_DIM
GLA_WIDTH = GLA_HEADS * GLA_V_DIM
GLA_GATE_RANK = 16
GLA_GATE_TAU = 16.0
D_FF = 2816
RMS_EPS = 1e-6
LAM_INIT = 0.8 - 0.6 * math.exp(-0.3 * 0)

LOG2E = 1.4426950408889634
Q_SCALE = DA_QK_DIM ** -0.5 * LOG2E
GLA_Q_SCALE = GLA_K_DIM ** -0.5
LANE = 128
SUB = 8
KEY_TILE = 512
SCORE_STRIP = 256
GLA_TILE = 512
CUMSUM_ROWS = 128
TOK_TILE = 1024
MLP_TILE = 512
FF_CHUNK = 256
CONV_SLOTS = 4
NAT_COLS = 2048 + LANE
VMEM_LIMIT = 56 * 1024 * 1024

_NT = (((1,), (1,)), ((), ()))
_TN = (((0,), (0,)), ((), ()))


def _dot(a, b):
    return jnp.dot(a, b, preferred_element_type=F32)


def _dot_nt(a, b):
    return lax.dot_general(a, b, _NT, preferred_element_type=F32)


def _dot_tn(a, b):
    return lax.dot_general(a, b, _TN, preferred_element_type=F32)


def _rms(x, w):
    return x * lax.rsqrt(jnp.mean(x * x, axis=-1, keepdims=True) + RMS_EPS) * w


def _log_sigmoid(z):
    return jnp.minimum(z, 0.0) - jnp.log(1.0 + jnp.exp(-jnp.abs(z)))


def _silu(g):
    hg = 0.5 * g
    return hg * jnp.tanh(hg) + hg


def _split3(x):
    hi = x.astype(BF16)
    r = x - hi.astype(F32)
    mid = r.astype(BF16)
    lo = (r - mid.astype(F32)).astype(BF16)
    return hi, mid, lo


def _cumsum_rows(tri, x):
    hi, mid, lo = _split3(x)
    return _dot(tri, hi) + _dot(tri, mid) + _dot(tri, lo)


def _lam(lam4):
    s1 = jnp.sum(lam4[0:1] * lam4[1:2], axis=-1, keepdims=True)
    s2 = jnp.sum(lam4[2:3] * lam4[3:4], axis=-1, keepdims=True)
    return jnp.exp(s1) - jnp.exp(s2) + LAM_INIT


def _head_mask(x, h, width):
    lane = lax.broadcasted_iota(jnp.int32, x.shape, 1)
    return jnp.where((lane >= h * width) & (lane < (h + 1) * width), x, jnp.zeros_like(x))


def _meta_kernel(meta_ref, n1_ref, wnat_ref, wt_ref, gw_ref, gb_ref, lam4_ref, c0_ref, subw_ref,
                 gnw_ref, smask_ref, wout_ref, n2_ref, wup_ref,
                 k_out, v_out, st_out, ut_out):
    m = N_META
    x = meta_ref[...]
    ub = _rms(x, n1_ref[...]).astype(BF16)
    nat = _dot(ub, wnat_ref[...])
    qv = _dot_nt(ub, wt_ref[...])
    kb = nat[:, 0:512].astype(BF16)
    qb = (qv[:, 0:512] * Q_SCALE).astype(BF16)
    vb = qv[:, 512:1024].astype(BF16)
    k_out[...] = kb
    v_out[...] = vb

    lam = _lam(lam4_ref[...])
    ii = lax.broadcasted_iota(jnp.int32, (m, m), 0)
    jj = lax.broadcasted_iota(jnp.int32, (m, m), 1)
    dist = jnp.abs(ii - jj).astype(F32)

    def softmax2(s):
        p = jnp.exp2(s - jnp.max(s, axis=-1, keepdims=True))
        return p / jnp.sum(p, axis=-1, keepdims=True)

    o_da = []
    for h in range(DA_HEADS):
        bias = -c0_ref[h] * dist
        p1 = softmax2(_dot_nt(_head_mask(qb, 2 * h, DA_QK_DIM), kb) + bias)
        p2 = softmax2(_dot_nt(_head_mask(qb, 2 * h + 1, DA_QK_DIM), kb) + bias)
        o = _dot((p1 - lam * p2).astype(BF16), vb[:, h * DA_V_DIM:(h + 1) * DA_V_DIM])
        o_da.append(_rms(o, subw_ref[...]) * (1.0 - LAM_INIT))
    o_da = jnp.concatenate(o_da, axis=1)

    gq = nat[:, 512:768].astype(BF16).astype(F32)
    gk = nat[:, 768:1024].astype(BF16).astype(F32)
    gv = nat[:, 1024:1536].astype(BF16)
    gr = nat[:, 1536:2048].astype(BF16).astype(F32)
    g16 = nat[:, 2048:NAT_COLS].astype(BF16)
    la = _log_sigmoid(_dot(g16, gw_ref[...]) + gb_ref[...]) / GLA_GATE_TAU
    tri = (jj <= ii).astype(BF16)
    b = _cumsum_rows(tri, la)
    bl = b[m - 1:m]
    qd = (gq * GLA_Q_SCALE * jnp.exp(b)).astype(BF16)
    ki = (gk * jnp.exp(-b)).astype(BF16)
    kd = (gk * jnp.exp(bl - b)).astype(BF16)
    o_gla = []
    for h in range(GLA_HEADS):
        a = _dot_nt(_head_mask(qd, h, GLA_K_DIM), ki)
        a = jnp.where(jj <= ii, a, 0.0).astype(BF16)
        o = _dot(a, gv[:, h * GLA_V_DIM:(h + 1) * GLA_V_DIM])
        o_gla.append(_rms(o, gnw_ref[...]))
    o_gla = jnp.concatenate(o_gla, axis=1) * _silu(gr)
    st_out[...] = smask_ref[...] * _dot_tn(gv, kd)

    mix = jnp.concatenate([o_da, o_gla], axis=1).astype(BF16)
    h1 = x + _dot(mix, wout_ref[...])
    u = _dot(_rms(h1, n2_ref[...]).astype(BF16), wup_ref[...])
    ut_out[...] = u[m - 8:m]


def _proj_kernel(x_ref, n1_ref, wnat_ref, wt_ref, gw_ref, gb_ref,
                 k_out, qt_out, vt_out, gq_out, gk_out, gv_out, gr_out, la_out):
    ub = _rms(x_ref[0], n1_ref[...]).astype(BF16)
    nat = _dot(ub, wnat_ref[...])
    k_out[0] = nat[:, 0:512].astype(BF16)
    gq_out[0] = nat[:, 512:768].astype(BF16)
    gk_out[0] = nat[:, 768:1024].astype(BF16)
    gv_out[0] = nat[:, 1024:1536].astype(BF16)
    gr_out[0] = nat[:, 1536:2048].astype(BF16)
    g16 = nat[:, 2048:NAT_COLS].astype(BF16)
    la_out[0] = _log_sigmoid(_dot(g16, gw_ref[...]) + gb_ref[...]) / GLA_GATE_TAU
    tt = _dot_nt(wt_ref[...], ub)
    qt_out[0] = (tt[0:512] * Q_SCALE).astype(BF16)
    vt_out[0] = tt[512:1024].astype(BF16)


def _attn_kernel(c0_ref, lam4_ref, qt_ref, k_ref, vt_ref, augk_ref, augq_ref, bdiag_ref,
                 kmeta_ref, vtmeta_ref, subw_ref, o_ref,
                 qrhs0_ref, qrhs1_ref, m0_ref, m1_ref, l0_ref, l1_ref, acc0_ref, acc1_ref,
                 sa_ref, sb_ref, mxa_ref, mxb_ref):
    c0 = c0_ref[pl.program_id(1)]
    n_super = k_ref.shape[1] // Q_SUPER
    assert n_super % 2 == 0
    states = ((qrhs0_ref, m0_ref, l0_ref, acc0_ref), (qrhs1_ref, m1_ref, l1_ref, acc1_ref))
    lam = _lam(lam4_ref[...])

    def softmax_pv(st, s, mx, vt, c, lanes):
        _, m_ref, l_ref, acc_ref = st
        m_old = m_ref[:, lanes]
        m_new = jnp.maximum(m_old, mx + c)
        alpha = jnp.exp2(m_old - m_new)
        p = jnp.exp2(s - (m_new - c))
        l_ref[:, lanes] = alpha * l_ref[:, lanes] + jnp.sum(p, axis=0, keepdims=True)
        pb = p.astype(BF16)
        if pb.shape[0] < vt.shape[1]:
            pb = jnp.concatenate([pb, jnp.zeros((vt.shape[1] - pb.shape[0], pb.shape[1]), BF16)], axis=0)
        acc_ref[:, lanes] = alpha * acc_ref[:, lanes] + _dot(vt, pb)
        m_ref[:, lanes] = m_new

    n_strips = 2 * Q_SUPER // SCORE_STRIP
    per_map = Q_SUPER // SCORE_STRIP
    strips = [slice(n * SCORE_STRIP, (n + 1) * SCORE_STRIP) for n in range(n_strips)]

    def query_block(n):
        return (n % per_map) * SCORE_STRIP // DIAG_TILE

    def key_lhs(j):
        k0 = pl.multiple_of(j * KEY_TILE, KEY_TILE)
        return jnp.concatenate([k_ref[0, pl.ds(k0, KEY_TILE), :], augk_ref[0]], axis=1)

    def score_strip(st, lhs, s_ref, mx_ref, n):
        s = _dot(lhs, st[0][:, strips[n]])
        s_ref[:, strips[n]] = s
        mx_ref[:, strips[n]] = jnp.max(s, axis=0, keepdims=True)

    def tile_step(st, q0, j, s_ref, mx_ref, diag_block, nxt=None):
        k0 = pl.multiple_of(j * KEY_TILE, KEY_TILE)
        vt = vt_ref[0, :, pl.ds(k0, KEY_TILE)]
        c = -c0 * (q0 - k0).astype(F32)
        todo = []
        if nxt is not None:
            jn, sn_ref, mxn_ref, todo = nxt
            todo = list(todo)
            lhs_n = key_lhs(jn)
        for n, lanes in enumerate(strips):
            if todo:
                score_strip(st, lhs_n, sn_ref, mxn_ref, todo.pop(0))
            if diag_block is None or query_block(n) > diag_block:
                softmax_pv(st, s_ref[:, lanes], mx_ref[:, lanes], vt, c, lanes)
            elif query_block(n) == diag_block:
                sub = (n % per_map) * SCORE_STRIP % DIAG_TILE
                rows = KEY_TILE // 2 if sub == 0 else KEY_TILE
                b0 = (n // per_map) * DIAG_TILE + sub
                s = s_ref[0:rows, lanes] + bdiag_ref[0, 0:rows, b0:b0 + SCORE_STRIP]
                softmax_pv(st, s, jnp.max(s, axis=0, keepdims=True), vt[:, 0:rows], c, lanes)
        assert not todo

    every = list(range(n_strips))
    n_diag = Q_SUPER // KEY_TILE
    assert n_diag == 2 and KEY_TILE == DIAG_TILE

    def begin(qs, st):
        qrhs_ref, m_ref, l_ref, acc_ref = st
        q0 = pl.multiple_of(qs * Q_SUPER, Q_SUPER)
        qt = qt_ref[0, :, pl.ds(q0, Q_SUPER)]
        zeros = jnp.zeros((DA_QK_DIM, Q_SUPER), BF16)
        qrhs_ref[0:64, 0:Q_SUPER] = qt[0:64]
        qrhs_ref[64:128, 0:Q_SUPER] = zeros
        qrhs_ref[0:64, Q_SUPER:2 * Q_SUPER] = zeros
        qrhs_ref[64:128, Q_SUPER:2 * Q_SUPER] = qt[64:128]
        qrhs_ref[128:256, 0:Q_SUPER] = augq_ref[0]
        qrhs_ref[128:256, Q_SUPER:2 * Q_SUPER] = augq_ref[0]
        m_ref[...] = jnp.full(m_ref.shape, -jnp.inf, F32)
        l_ref[...] = jnp.zeros(l_ref.shape, F32)
        acc_ref[...] = jnp.zeros(acc_ref.shape, F32)
        s_meta = _dot(jnp.concatenate([kmeta_ref[0], augk_ref[0, 0:N_META, :]], axis=1), qrhs_ref[...])
        lhs0 = key_lhs(0)
        c_meta = -c0 * (q0 + N_META).astype(F32)
        for n, lanes in enumerate(strips):
            score_strip(st, lhs0, sa_ref, mxa_ref, n)
            sm = s_meta[:, lanes]
            softmax_pv(st, sm, jnp.max(sm, axis=0, keepdims=True), vtmeta_ref[0], c_meta, lanes)

    def main(qs, st):
        q0 = pl.multiple_of(qs * Q_SUPER, Q_SUPER)
        n_past = qs * n_diag

        def pair_body(jj, carry):
            j = 2 * jj
            tile_step(st, q0, j, sa_ref, mxa_ref, None, (j + 1, sb_ref, mxb_ref, every))
            tile_step(st, q0, j + 1, sb_ref, mxb_ref, None, (j + 2, sa_ref, mxa_ref, every))
            return carry

        lax.fori_loop(0, qs, pair_body, 0)
        later = [n for n in every if query_block(n) == 1]
        tile_step(st, q0, n_past, sa_ref, mxa_ref, 0, (n_past + 1, sb_ref, mxb_ref, later))
        tile_step(st, q0, n_past + 1, sb_ref, mxb_ref, 1)

    def finish(qs, st):
        _, _, l_ref, acc_ref = st
        q0 = pl.multiple_of(qs * Q_SUPER, Q_SUPER)
        inv = 1.0 / l_ref[...]
        acc = acc_ref[...]
        o = (acc[:, 0:Q_SUPER] * inv[:, 0:Q_SUPER]
             - lam * (acc[:, Q_SUPER:2 * Q_SUPER] * inv[:, Q_SUPER:2 * Q_SUPER]))
        y = o * lax.rsqrt(jnp.mean(o * o, axis=0, keepdims=True) + RMS_EPS) * subw_ref[...] * (1.0 - LAM_INIT)
        o_ref[0, pl.ds(q0, Q_SUPER), :] = y.T.astype(BF16)

    begin(0, states[0])

    def two_super_blocks(i, carry):
        qa = 2 * i
        main(qa, states[0])
        begin(qa + 1, states[1])
        finish(qa, states[0])
        main(qa + 1, states[1])
        begin(jnp.minimum(qa + 2, n_super - 1), states[0])
        finish(qa + 1, states[1])
        return carry

    lax.fori_loop(0, n_super // 2, two_super_blocks, 0)


def _gla_kernel(gq_ref, gk_ref, gv_ref, gr_ref, la_ref, tri_ref, st0_ref, gnw_ref, smask_ref,
                o_ref, st_ref):
    @pl.when(pl.program_id(1) == 0)
    def _():
        st_ref[...] = st0_ref[...]

    tri = tri_ref[...]
    la = la_ref[0]
    b = jnp.concatenate([_cumsum_rows(tri, la[g:g + CUMSUM_ROWS]) for g in range(0, GLA_TILE, CUMSUM_ROWS)],
                        axis=0)
    ci = lax.broadcasted_iota(jnp.int32, (GLA_HEADS * CHUNK, CHUNK), 0) % CHUNK
    si = lax.broadcasted_iota(jnp.int32, (GLA_HEADS * CHUNK, CHUNK), 1)
    causal = si <= ci
    smask = smask_ref[...]
    gnw = gnw_ref[...]

    for c in range(GLA_TILE // CHUNK):
        sl = slice(c * CHUNK, (c + 1) * CHUNK)
        bc = b[sl]
        bl = bc[CHUNK - 1:CHUNK]
        q = gq_ref[0, sl, :].astype(F32)
        k = gk_ref[0, sl, :].astype(F32)
        v = gv_ref[0, sl, :]
        qd = (q * GLA_Q_SCALE * jnp.exp(bc)).astype(BF16)
        ki = (k * jnp.exp(-bc)).astype(BF16)
        kd = (k * jnp.exp(bl - bc)).astype(BF16)
        qm = jnp.concatenate([_head_mask(qd, hh, GLA_K_DIM) for hh in range(GLA_HEADS)], axis=0)
        a = jnp.where(causal, _dot_nt(qm, ki), 0.0).astype(BF16)
        st = st_ref[...]
        o_inter = _dot_nt(qd, st.astype(BF16))
        r = gr_ref[0, sl, :].astype(F32)
        for hh in range(GLA_HEADS):
            vs = slice(hh * GLA_V_DIM, (hh + 1) * GLA_V_DIM)
            o = _dot(a[hh * CHUNK:(hh + 1) * CHUNK], v[:, vs]) + o_inter[:, vs]
            o_ref[0, sl, vs] = (_rms(o, gnw) * _silu(r[:, vs])).astype(BF16)
        st_ref[...] = jnp.exp(bl) * st + smask * _dot_tn(v, kd)


def _mlp_kernel(x_ref, oda_ref, ogla_ref, wout_ref, n2_ref, wup_ref, cw_ref, cb_ref, wdown_ref,
                fnw_ref, ut_ref, o_ref, carry_ref, ubuf_ref, act_ref):
    @pl.when(pl.program_id(1) == 0)
    def _():
        carry_ref[...] = ut_ref[...]

    t = x_ref.shape[1]
    mix = jnp.concatenate([oda_ref[0], ogla_ref[0]], axis=1)
    h1 = x_ref[0] + _dot(mix, wout_ref[...])
    o_ref[0] = h1
    xn = _rms(h1, n2_ref[...]).astype(BF16)

    def stage(col0, slot):
        cols = slice(col0, col0 + FF_CHUNK)
        u = _dot(xn, wup_ref[:, cols])
        ubuf_ref[slot, 0:SUB, :] = carry_ref[:, cols]
        ubuf_ref[slot, SUB:SUB + t, :] = u
        carry_ref[:, cols] = u[t - SUB:t]

    def conv(col0, slot):
        cols = slice(col0, col0 + FF_CHUNK)
        w = cw_ref[:, cols]
        return (w[0:1] * ubuf_ref[slot, SUB - 2:SUB - 2 + t, :] + w[1:2] * ubuf_ref[slot, SUB - 1:SUB - 1 + t, :]
                + w[2:3] * ubuf_ref[slot, SUB:SUB + t, :] + cb_ref[:, cols])

    def stage_chunk(j):
        stage(j * FF_CHUNK, (2 * j) % CONV_SLOTS)
        stage(D_FF + j * FF_CHUNK, (2 * j + 1) % CONV_SLOTS)

    n_chunks = D_FF // FF_CHUNK
    stage_chunk(0)
    for j in range(n_chunks):
        if j + 1 < n_chunks:
            stage_chunk(j + 1)
        val = conv(j * FF_CHUNK, (2 * j) % CONV_SLOTS)
        gate = conv(D_FF + j * FF_CHUNK, (2 * j + 1) % CONV_SLOTS)
        act_ref[:, j * FF_CHUNK:(j + 1) * FF_CHUNK] = (val * _silu(gate)).astype(BF16)
    o_ref[0] = _rms(o_ref[0] + _dot(act_ref[...], wdown_ref[...]), fnw_ref[...])


def _bf16_pieces(c):
    out = []
    for _ in range(3):
        p = np.float32(c).astype(BF16).astype(np.float32)
        out.append(p)
        c = np.float32(c) - p
    return out


@functools.lru_cache(maxsize=None)
def _attn_constants():
    assert KEY_TILE == DIAG_TILE == 512 and Q_SUPER <= 1024
    slopes = 2.0 ** (-8.0 * np.arange(1, DA_HEADS + 1, dtype=np.float64) / DA_HEADS)
    c0 = (slopes * LOG2E).astype(np.float32)
    augk = np.zeros((DA_HEADS, KEY_TILE, LANE), np.float32)
    augq = np.zeros((DA_HEADS, LANE, Q_SUPER), np.float32)
    j = np.arange(KEY_TILE)
    iq = np.arange(Q_SUPER)
    for h in range(DA_HEADS):
        a = _bf16_pieces(c0[h])
        for n in range(3):
            augk[h, :, n] = -a[n]
            augq[h, n, :] = iq % 256
            augk[h, :, 3 + n] = -a[n]
            augq[h, 3 + n, :] = (iq // 256) * 256
            augk[h, :, 6 + n] = j % 256
            augq[h, 6 + n, :] = a[n]
            augk[h, :, 9 + n] = (j // 256) * 256
            augq[h, 9 + n, :] = a[n]
    jj, ii = np.meshgrid(np.arange(KEY_TILE), np.arange(DIAG_TILE), indexing="ij")
    visible = (jj // CHUNK) <= (ii // CHUNK)
    bdiag = np.zeros((DA_HEADS, KEY_TILE, 2 * DIAG_TILE), np.float32)
    for h in range(DA_HEADS):
        corr = np.where(jj > ii, -2.0 * c0[h] * (jj - ii), 0.0)
        tile = np.where(visible, corr, -np.inf).astype(np.float32)
        bdiag[h] = np.concatenate([tile, tile], axis=1)
    return c0, augk, augq, bdiag


@functools.lru_cache(maxsize=None)
def _gla_constants():
    r = np.arange(CUMSUM_ROWS)
    tri = ((r[:, None] // CHUNK == r[None, :] // CHUNK) & (r[None, :] <= r[:, None])).astype(np.float32)
    rows = np.arange(GLA_WIDTH)[:, None] // GLA_V_DIM
    cols = np.arange(GLA_KW)[None, :] // GLA_K_DIM
    smask = (rows == cols).astype(np.float32)
    return tri, smask


def _full(shape):
    return pl.BlockSpec(shape, lambda *_: (0,) * len(shape))


def _resident(shape):
    return pl.BlockSpec(shape, lambda *_: (0,) * len(shape), pipeline_mode=pl.Buffered(1))


def _params(sem, flags=None):
    return pltpu.CompilerParams(dimension_semantics=sem, vmem_limit_bytes=VMEM_LIMIT, flags=flags)


def kernel(x, meta_tokens, norm1_w, w_in, lambda_q1, lambda_k1, lambda_q2, lambda_k2, da_subln_w,
           gla_gate_w, gla_gate_b, gla_norm_w, w_out, norm2_w, w_up, conv_w, conv_b, w_down,
           final_norm_w):
    bsz, seq, _ = x.shape
    assert seq % Q_SUPER == 0 and seq % TOK_TILE == 0 and seq % GLA_TILE == 0 and seq % MLP_TILE == 0
    assert norm1_w.shape[0] == 1
    nq = seq // Q_SUPER

    w = w_in[0]
    wq, wk, wv = w[:, 0:512], w[:, 512:1024], w[:, 1024:1536]
    wrest, wlr = w[:, 1536:3072], w[:, 3072:3088]
    wnat = jnp.concatenate([wk, wrest, jnp.pad(wlr, ((0, 0), (0, LANE - GLA_GATE_RANK)))], axis=1).astype(BF16)
    wt = jnp.concatenate([wq, wv], axis=1).T.astype(BF16)
    gw = jnp.pad(gla_gate_w[0], ((0, LANE - GLA_GATE_RANK), (0, 0))).astype(BF16)
    gb = gla_gate_b[0][None].astype(F32)
    n1 = norm1_w[0][None]
    n2 = norm2_w[0][None]
    fnw = final_norm_w[None]
    gnw = gla_norm_w[0][None]
    subw_row = da_subln_w[0][None]
    subw_col = jnp.broadcast_to(da_subln_w[0][:, None], (DA_V_DIM, Q_SUPER))
    lam4 = jnp.concatenate([lambda_q1, lambda_k1, lambda_q2, lambda_k2], axis=0)
    wout = w_out[0].astype(BF16)
    wup = w_up[0].astype(BF16)
    wdown = w_down[0].astype(BF16)
    cw = conv_w[0]
    cb = conv_b[0][None]

    c0_np, augk_np, augq_np, bdiag_np = _attn_constants()
    tri_np, smask_np = _gla_constants()
    c0 = jnp.asarray(c0_np)
    augk = jnp.asarray(augk_np, BF16)
    augq = jnp.asarray(augq_np, BF16)
    bdiag = jnp.asarray(bdiag_np)
    tri = jnp.asarray(tri_np, BF16)
    smask = jnp.asarray(smask_np)

    smem = pl.BlockSpec(memory_space=pltpu.SMEM)
    vmem = pl.BlockSpec(memory_space=pltpu.VMEM)

    k_meta, v_meta, st0, utail = pl.pallas_call(
        _meta_kernel,
        out_shape=(jax.ShapeDtypeStruct((N_META, 512), BF16),
                   jax.ShapeDtypeStruct((N_META, 512), BF16),
                   jax.ShapeDtypeStruct((GLA_WIDTH, GLA_KW), F32),
                   jax.ShapeDtypeStruct((8, 2 * D_FF), F32)),
        in_specs=[vmem, vmem, vmem, vmem, vmem, vmem, vmem, smem, vmem, vmem, vmem, vmem, vmem, vmem],
        out_specs=(vmem, vmem, vmem, vmem),
        compiler_params=pltpu.CompilerParams(vmem_limit_bytes=VMEM_LIMIT),
        name="meta_mixer",
    )(meta_tokens, n1, wnat, wt, gw, gb, lam4, c0, subw_row, gnw, smask, wout, n2, wup)
    kmeta = k_meta.reshape(N_META, DA_HEADS, LANE).transpose(1, 0, 2)
    vtmeta = jnp.pad(v_meta.reshape(N_META, DA_HEADS, DA_V_DIM).transpose(1, 2, 0),
                     ((0, 0), (0, 0), (0, LANE - N_META)))

    nt = seq // TOK_TILE
    tok = lambda width: pl.BlockSpec((1, TOK_TILE, width), lambda b, t: (b, t, 0))
    tokt = pl.BlockSpec((1, 512, TOK_TILE), lambda b, t: (b, 0, t))
    kcat, qt, vt, gq, gk, gv, gr, la = pl.pallas_call(
        _proj_kernel,
        grid=(bsz, nt),
        out_shape=(jax.ShapeDtypeStruct((bsz, seq, 512), BF16),
                   jax.ShapeDtypeStruct((bsz, 512, seq), BF16),
                   jax.ShapeDtypeStruct((bsz, 512, seq), BF16),
                   jax.ShapeDtypeStruct((bsz, seq, GLA_KW), BF16),
                   jax.ShapeDtypeStruct((bsz, seq, GLA_KW), BF16),
                   jax.ShapeDtypeStruct((bsz, seq, GLA_WIDTH), BF16),
                   jax.ShapeDtypeStruct((bsz, seq, GLA_WIDTH), BF16),
                   jax.ShapeDtypeStruct((bsz, seq, GLA_KW), F32)),
        in_specs=[tok(D_MODEL), _full((1, D_MODEL)), _resident((D_MODEL, NAT_COLS)),
                  _resident((1024, D_MODEL)), _full((LANE, GLA_KW)), _full((1, GLA_KW))],
        out_specs=(tok(512), tokt, tokt, tok(GLA_KW), tok(GLA_KW), tok(GLA_WIDTH), tok(GLA_WIDTH),
                   tok(GLA_KW)),
        compiler_params=_params(("parallel", "arbitrary")),
        name="in_proj",
    )(x, n1, wnat, wt, gw, gb)

    per_head = lambda shape: pl.BlockSpec((1,) + shape, lambda b, h: (h, 0, 0))
    row = pltpu.VMEM((1, 2 * Q_SUPER), F32)
    o_da = pl.pallas_call(
        _attn_kernel,
        grid=(bsz, DA_HEADS),
        out_shape=jax.ShapeDtypeStruct((bsz, seq, DA_WIDTH), BF16),
        in_specs=[smem, _full((4, DA_QK_DIM)),
                  pl.BlockSpec((1, 2 * DA_QK_DIM, seq), lambda b, h: (b, h, 0)),
                  pl.BlockSpec((1, seq, LANE), lambda b, h: (b, 0, h)),
                  pl.BlockSpec((1, DA_V_DIM, seq), lambda b, h: (b, h, 0)),
                  per_head((KEY_TILE, LANE)), per_head((LANE, Q_SUPER)), per_head((KEY_TILE, 2 * DIAG_TILE)),
                  per_head((N_META, LANE)), per_head((DA_V_DIM, LANE)), _full((DA_V_DIM, Q_SUPER))],
        out_specs=pl.BlockSpec((1, seq, DA_V_DIM), lambda b, h: (b, 0, h)),
        scratch_shapes=[pltpu.VMEM((2 * LANE, 2 * Q_SUPER), BF16),
                        pltpu.VMEM((2 * LANE, 2 * Q_SUPER), BF16),
                        row, row, row, row,
                        pltpu.VMEM((DA_V_DIM, 2 * Q_SUPER), F32),
                        pltpu.VMEM((DA_V_DIM, 2 * Q_SUPER), F32),
                        pltpu.VMEM((KEY_TILE, 2 * Q_SUPER), F32),
                        pltpu.VMEM((KEY_TILE, 2 * Q_SUPER), F32),
                        row, row],
        compiler_params=_params(("parallel", "parallel")),
        name="diff_attn",
    )(c0, lam4, qt, kcat, vt, augk, augq, bdiag, kmeta, vtmeta, subw_col)

    ng = seq // GLA_TILE
    gtok = lambda width: pl.BlockSpec((1, GLA_TILE, width), lambda b, t: (b, t, 0))
    o_gla = pl.pallas_call(
        _gla_kernel,
        grid=(bsz, ng),
        out_shape=jax.ShapeDtypeStruct((bsz, seq, GLA_WIDTH), BF16),
        in_specs=[gtok(GLA_KW), gtok(GLA_KW), gtok(GLA_WIDTH), gtok(GLA_WIDTH), gtok(GLA_KW),
                  _full((CUMSUM_ROWS, CUMSUM_ROWS)), _full((GLA_WIDTH, GLA_KW)), _full((1, GLA_V_DIM)),
                  _full((GLA_WIDTH, GLA_KW))],
        out_specs=gtok(GLA_WIDTH),
        scratch_shapes=[pltpu.VMEM((GLA_WIDTH, GLA_KW), F32)],
        compiler_params=_params(("parallel", "arbitrary")),
        name="gla",
    )(gq, gk, gv, gr, la, tri, st0, gnw, smask)

    mtok = lambda width: pl.BlockSpec((1, MLP_TILE, width), lambda b, t: (b, t, 0))
    out = pl.pallas_call(
        _mlp_kernel,
        grid=(bsz, seq // MLP_TILE),
        out_shape=jax.ShapeDtypeStruct((bsz, seq, D_MODEL), F32),
        in_specs=[mtok(D_MODEL), mtok(DA_WIDTH), mtok(GLA_WIDTH), _resident((D_MODEL, D_MODEL)),
                  _full((1, D_MODEL)), _resident((D_MODEL, 2 * D_FF)), _full((3, 2 * D_FF)),
                  _full((1, 2 * D_FF)), _resident((D_FF, D_MODEL)), _full((1, D_MODEL)),
                  _full((8, 2 * D_FF))],
        out_specs=mtok(D_MODEL),
        scratch_shapes=[pltpu.VMEM((8, 2 * D_FF), F32),
                        pltpu.VMEM((CONV_SLOTS, MLP_TILE + SUB, FF_CHUNK), F32),
                        pltpu.VMEM((MLP_TILE, D_FF), BF16)],
        compiler_params=_params(("parallel", "arbitrary")),
        name="out_proj_mlp",
    )(x, o_da, o_gla, wout, n2, wup, cw, cb, wdown, fnw, utail)
    return out
```

```python
import functools
import math

import numpy as np
import jax
import jax.numpy as jnp
from jax import lax
from jax.experimental import pallas as pl
from jax.experimental.pallas import tpu as pltpu

F32 = jnp.float32
BF16 = jnp.bfloat16

D_MODEL = 1024
N_META = 16
CHUNK = 64
Q_SUPER = 1024
DIAG_TILE = 512
DA_HEADS = 4
DA_QK_DIM = 64
DA_V_DIM = 128
DA_WIDTH = DA_HEADS * DA_V_DIM
GLA_HEADS = 4
GLA_K_DIM = 64
GLA_V_DIM = 128
GLA_KW = GLA_HEADS * GLA_K_DIM
GLA_WIDTH = GLA_HEADS * GLA_V_DIM
GLA_GATE_RANK = 16
GLA_GATE_TAU = 16.0
D_FF = 2816
RMS_EPS = 1e-6
LAM_INIT = 0.8 - 0.6 * math.exp(-0.3 * 0)

LOG2E = 1.4426950408889634
Q_SCALE = DA_QK_DIM ** -0.5 * LOG2E
GLA_Q_SCALE = GLA_K_DIM ** -0.5
LANE = 128
SUB = 8
KEY_TILE = 512
SCORE_STRIP = 256
GLA_TILE = 1024
CUMSUM_ROWS = 128
TOK_TILE = 1024
MLP_TILE = 512
FF_CHUNK = 256
CONV_SLOTS = 4
NAT_COLS = 2048 + LANE
VMEM_LIMIT = 56 * 1024 * 1024

_NT = (((1,), (1,)), ((), ()))
_TN = (((0,), (0,)), ((), ()))


def _dot(a, b):
    return jnp.dot(a, b, preferred_element_type=F32)


def _dot_nt(a, b):
    return lax.dot_general(a, b, _NT, preferred_element_type=F32)


def _dot_tn(a, b):
    return lax.dot_general(a, b, _TN, preferred_element_type=F32)


def _rms(x, w):
    return x * lax.rsqrt(jnp.mean(x * x, axis=-1, keepdims=True) + RMS_EPS) * w


def _log_sigmoid(z):
    return jnp.minimum(z, 0.0) - jnp.log(1.0 + jnp.exp(-jnp.abs(z)))


def _silu(g):
    hg = 0.5 * g
    return hg * jnp.tanh(hg) + hg


def _split3(x):
    hi = x.astype(BF16)
    r = x - hi.astype(F32)
    mid = r.astype(BF16)
    lo = (r - mid.astype(F32)).astype(BF16)
    return hi, mid, lo


def _cumsum_rows(tri, x):
    hi, mid, lo = _split3(x)
    return _dot(tri, hi) + _dot(tri, mid) + _dot(tri, lo)


def _lam(lam4):
    s1 = jnp.sum(lam4[0:1] * lam4[1:2], axis=-1, keepdims=True)
    s2 = jnp.sum(lam4[2:3] * lam4[3:4], axis=-1, keepdims=True)
    return jnp.exp(s1) - jnp.exp(s2) + LAM_INIT


def _head_mask(x, h, width):
    lane = lax.broadcasted_iota(jnp.int32, x.shape, 1)
    return jnp.where((lane >= h * width) & (lane < (h + 1) * width), x, jnp.zeros_like(x))


def _meta_kernel(meta_ref, n1_ref, wnat_ref, wt_ref, gw_ref, gb_ref, lam4_ref, c0_ref, subw_ref,
                 gnw_ref, wout_ref, n2_ref, wup_ref,
                 k_out, v_out, st_out, ut_out):
    m = N_META
    x = meta_ref[...]
    ub = _rms(x, n1_ref[...]).astype(BF16)
    nat = _dot(ub, wnat_ref[...])
    qv = _dot_nt(ub, wt_ref[...])
    kb = nat[:, 0:512].astype(BF16)
    qb = (qv[:, 0:512] * Q_SCALE).astype(BF16)
    vb = qv[:, 512:1024].astype(BF16)
    k_out[...] = kb
    v_out[...] = vb

    lam = _lam(lam4_ref[...])
    ii = lax.broadcasted_iota(jnp.int32, (m, m), 0)
    jj = lax.broadcasted_iota(jnp.int32, (m, m), 1)
    dist = jnp.abs(ii - jj).astype(F32)

    def softmax2(s):
        p = jnp.exp2(s - jnp.max(s, axis=-1, keepdims=True))
        return p / jnp.sum(p, axis=-1, keepdims=True)

    o_da = []
    for h in range(DA_HEADS):
        bias = -c0_ref[h] * dist
        p1 = softmax2(_dot_nt(_head_mask(qb, 2 * h, DA_QK_DIM), kb) + bias)
        p2 = softmax2(_dot_nt(_head_mask(qb, 2 * h + 1, DA_QK_DIM), kb) + bias)
        o = _dot((p1 - lam * p2).astype(BF16), vb[:, h * DA_V_DIM:(h + 1) * DA_V_DIM])
        o_da.append(_rms(o, subw_ref[...]) * (1.0 - LAM_INIT))
    o_da = jnp.concatenate(o_da, axis=1)

    gq = nat[:, 512:768].astype(BF16).astype(F32)
    gk = nat[:, 768:1024].astype(BF16).astype(F32)
    gv = nat[:, 1024:1536].astype(BF16)
    gr = nat[:, 1536:2048].astype(BF16).astype(F32)
    g16 = nat[:, 2048:NAT_COLS].astype(BF16)
    la = _log_sigmoid(_dot(g16, gw_ref[...]) + gb_ref[...]) / GLA_GATE_TAU
    tri = (jj <= ii).astype(BF16)
    b = _cumsum_rows(tri, la)
    bl = b[m - 1:m]
    qd = (gq * GLA_Q_SCALE * jnp.exp(b)).astype(BF16)
    ki = (gk * jnp.exp(-b)).astype(BF16)
    kd = (gk * jnp.exp(bl - b)).astype(BF16)
    o_gla = []
    for h in range(GLA_HEADS):
        a = _dot_nt(_head_mask(qd, h, GLA_K_DIM), ki)
        a = jnp.where(jj <= ii, a, 0.0).astype(BF16)
        o = _dot(a, gv[:, h * GLA_V_DIM:(h + 1) * GLA_V_DIM])
        o_gla.append(_rms(o, gnw_ref[...]))
    o_gla = jnp.concatenate(o_gla, axis=1) * _silu(gr)
    st_out[...] = _dot_tn(_stack_values(gv), _stack_heads(kd))

    mix = jnp.concatenate([o_da, o_gla], axis=1).astype(BF16)
    h1 = x + _dot(mix, wout_ref[...])
    u = _dot(_rms(h1, n2_ref[...]).astype(BF16), wup_ref[...])
    ut_out[...] = u[m - 8:m]


def _proj_kernel(x_ref, n1_ref, wnat_ref, wt_ref, gw_ref, gb_ref,
                 k_out, qt_out, vt_out, gq_out, gk_out, gv_out, gr_out, la_out):
    ub = _rms(x_ref[0], n1_ref[...]).astype(BF16)
    nat = _dot(ub, wnat_ref[...])
    k_out[0] = nat[:, 0:512].astype(BF16)
    gq_out[0] = nat[:, 512:768].astype(BF16)
    gk_out[0] = nat[:, 768:1024].astype(BF16)
    gv_out[0] = nat[:, 1024:1536].astype(BF16)
    gr_out[0] = nat[:, 1536:2048].astype(BF16)
    g16 = nat[:, 2048:NAT_COLS].astype(BF16)
    la_out[0] = _log_sigmoid(_dot(g16, gw_ref[...]) + gb_ref[...]) / GLA_GATE_TAU
    tt = _dot_nt(wt_ref[...], ub)
    qt_out[0] = (tt[0:512] * Q_SCALE).astype(BF16)
    vt_out[0] = tt[512:1024].astype(BF16)


def _attn_kernel(c0_ref, lam4_ref, qt_ref, k_ref, vt_ref, augk_ref, augq_ref, bdiag_ref,
                 kmeta_ref, vtmeta_ref, subw_ref, o_ref,
                 qrhs0_ref, qrhs1_ref, m0_ref, m1_ref, l0_ref, l1_ref, acc0_ref, acc1_ref,
                 sa_ref, sb_ref, mxa_ref, mxb_ref):
    c0 = c0_ref[pl.program_id(1)]
    n_super = k_ref.shape[1] // Q_SUPER
    assert n_super % 2 == 0
    states = ((qrhs0_ref, m0_ref, l0_ref, acc0_ref), (qrhs1_ref, m1_ref, l1_ref, acc1_ref))
    lam = _lam(lam4_ref[...])

    def softmax_pv(st, s, mx, vt, c, lanes):
        _, m_ref, l_ref, acc_ref = st
        m_old = m_ref[:, lanes]
        m_new = jnp.maximum(m_old, mx + c)
        alpha = jnp.exp2(m_old - m_new)
        p = jnp.exp2(s - (m_new - c))
        l_ref[:, lanes] = alpha * l_ref[:, lanes] + jnp.sum(p, axis=0, keepdims=True)
        pb = p.astype(BF16)
        if pb.shape[0] < vt.shape[1]:
            pb = jnp.concatenate([pb, jnp.zeros((vt.shape[1] - pb.shape[0], pb.shape[1]), BF16)], axis=0)
        acc_ref[:, lanes] = alpha * acc_ref[:, lanes] + _dot(vt, pb)
        m_ref[:, lanes] = m_new

    n_strips = 2 * Q_SUPER // SCORE_STRIP
    per_map = Q_SUPER // SCORE_STRIP
    strips = [slice(n * SCORE_STRIP, (n + 1) * SCORE_STRIP) for n in range(n_strips)]

    def query_block(n):
        return (n % per_map) * SCORE_STRIP // DIAG_TILE

    def key_lhs(j):
        k0 = pl.multiple_of(j * KEY_TILE, KEY_TILE)
        return jnp.concatenate([k_ref[0, pl.ds(k0, KEY_TILE), :], augk_ref[0]], axis=1)

    def score_strip(st, lhs, s_ref, mx_ref, n):
        s = _dot(lhs, st[0][:, strips[n]])
        s_ref[:, strips[n]] = s
        mx_ref[:, strips[n]] = jnp.max(s, axis=0, keepdims=True)

    def tile_step(st, q0, j, s_ref, mx_ref, diag_block, nxt=None):
        k0 = pl.multiple_of(j * KEY_TILE, KEY_TILE)
        vt = vt_ref[0, :, pl.ds(k0, KEY_TILE)]
        c = -c0 * (q0 - k0).astype(F32)
        todo = []
        if nxt is not None:
            jn, sn_ref, mxn_ref, todo = nxt
            todo = list(todo)
            lhs_n = key_lhs(jn)
        for n, lanes in enumerate(strips):
            if todo:
                score_strip(st, lhs_n, sn_ref, mxn_ref, todo.pop(0))
            if diag_block is None or query_block(n) > diag_block:
                softmax_pv(st, s_ref[:, lanes], mx_ref[:, lanes], vt, c, lanes)
            elif query_block(n) == diag_block:
                sub = (n % per_map) * SCORE_STRIP % DIAG_TILE
                rows = KEY_TILE // 2 if sub == 0 else KEY_TILE
                b0 = (n // per_map) * DIAG_TILE + sub
                s = s_ref[0:rows, lanes] + bdiag_ref[0, 0:rows, b0:b0 + SCORE_STRIP]
                softmax_pv(st, s, jnp.max(s, axis=0, keepdims=True), vt[:, 0:rows], c, lanes)
        assert not todo

    every = list(range(n_strips))
    n_diag = Q_SUPER // KEY_TILE
    assert n_diag == 2 and KEY_TILE == DIAG_TILE

    def begin(qs, st):
        qrhs_ref, m_ref, l_ref, acc_ref = st
        q0 = pl.multiple_of(qs * Q_SUPER, Q_SUPER)
        qt = qt_ref[0, :, pl.ds(q0, Q_SUPER)]
        zeros = jnp.zeros((DA_QK_DIM, Q_SUPER), BF16)
        qrhs_ref[0:64, 0:Q_SUPER] = qt[0:64]
        qrhs_ref[64:128, 0:Q_SUPER] = zeros
        qrhs_ref[0:64, Q_SUPER:2 * Q_SUPER] = zeros
        qrhs_ref[64:128, Q_SUPER:2 * Q_SUPER] = qt[64:128]
        qrhs_ref[128:256, 0:Q_SUPER] = augq_ref[0]
        qrhs_ref[128:256, Q_SUPER:2 * Q_SUPER] = augq_ref[0]
        m_ref[...] = jnp.full(m_ref.shape, -jnp.inf, F32)
        l_ref[...] = jnp.zeros(l_ref.shape, F32)
        acc_ref[...] = jnp.zeros(acc_ref.shape, F32)
        s_meta = _dot(jnp.concatenate([kmeta_ref[0], augk_ref[0, 0:N_META, :]], axis=1), qrhs_ref[...])
        lhs0 = key_lhs(0)
        c_meta = -c0 * (q0 + N_META).astype(F32)
        for n, lanes in enumerate(strips):
            score_strip(st, lhs0, sa_ref, mxa_ref, n)
            sm = s_meta[:, lanes]
            softmax_pv(st, sm, jnp.max(sm, axis=0, keepdims=True), vtmeta_ref[0], c_meta, lanes)

    def main(qs, st):
        q0 = pl.multiple_of(qs * Q_SUPER, Q_SUPER)
        n_past = qs * n_diag

        def pair_body(jj, carry):
            j = 2 * jj
            tile_step(st, q0, j, sa_ref, mxa_ref, None, (j + 1, sb_ref, mxb_ref, every))
            tile_step(st, q0, j + 1, sb_ref, mxb_ref, None, (j + 2, sa_ref, mxa_ref, every))
            return carry

        lax.fori_loop(0, qs, pair_body, 0)
        later = [n for n in every if query_block(n) == 1]
        tile_step(st, q0, n_past, sa_ref, mxa_ref, 0, (n_past + 1, sb_ref, mxb_ref, later))
        tile_step(st, q0, n_past + 1, sb_ref, mxb_ref, 1)

    def finish(qs, st):
        _, _, l_ref, acc_ref = st
        q0 = pl.multiple_of(qs * Q_SUPER, Q_SUPER)
        inv = 1.0 / l_ref[...]
        acc = acc_ref[...]
        o = (acc[:, 0:Q_SUPER] * inv[:, 0:Q_SUPER]
             - lam * (acc[:, Q_SUPER:2 * Q_SUPER] * inv[:, Q_SUPER:2 * Q_SUPER]))
        y = o * lax.rsqrt(jnp.mean(o * o, axis=0, keepdims=True) + RMS_EPS) * subw_ref[...] * (1.0 - LAM_INIT)
        o_ref[0, pl.ds(q0, Q_SUPER), :] = y.T.astype(BF16)

    begin(0, states[0])

    def two_super_blocks(i, carry):
        qa = 2 * i
        main(qa, states[0])
        begin(qa + 1, states[1])
        finish(qa, states[0])
        main(qa + 1, states[1])
        begin(jnp.minimum(qa + 2, n_super - 1), states[0])
        finish(qa + 1, states[1])
        return carry

    lax.fori_loop(0, n_super // 2, two_super_blocks, 0)


def _stack_heads(x):
    return jnp.concatenate([_head_mask(x, hh, GLA_K_DIM) for hh in range(GLA_HEADS)], axis=0)


def _stack_values(v):
    return jnp.concatenate([v[:, hh * GLA_V_DIM:(hh + 1) * GLA_V_DIM] for hh in range(GLA_HEADS)], axis=0)


def _gla_kernel(gq_ref, gk_ref, gv_ref, gr_ref, la_ref, tri_ref, st0_ref, gnw_ref,
                o_ref, st_ref, delta_ref, intra_ref, qm_ref):
    @pl.when(pl.program_id(1) == 0)
    def _():
        st_ref[...] = st0_ref[...]

    tri = tri_ref[...]
    la = la_ref[0]
    b = jnp.concatenate([_cumsum_rows(tri, la[g:g + CUMSUM_ROWS]) for g in range(0, GLA_TILE, CUMSUM_ROWS)],
                        axis=0)
    ci = lax.broadcasted_iota(jnp.int32, (GLA_HEADS * CHUNK, CHUNK), 0) % CHUNK
    si = lax.broadcasted_iota(jnp.int32, (GLA_HEADS * CHUNK, CHUNK), 1)
    causal = si <= ci
    gnw = gnw_ref[...]

    n_chunks = GLA_TILE // CHUNK
    chunk = lambda c: slice(c * CHUNK, (c + 1) * CHUNK)

    decay, scores = [], []
    for c in range(n_chunks):
        sl = chunk(c)
        bc = b[sl]
        bl = bc[CHUNK - 1:CHUNK]
        q = gq_ref[0, sl, :].astype(F32)
        k = gk_ref[0, sl, :].astype(F32)
        qd = (q * GLA_Q_SCALE * jnp.exp(bc)).astype(BF16)
        ki = (k * jnp.exp(-bc)).astype(BF16)
        kd = (k * jnp.exp(bl - bc)).astype(BF16)
        decay.append(jnp.exp(bl))
        qm = _stack_heads(qd)
        qm_ref[c] = qm
        scores.append(_dot_nt(qm, ki))
        delta_ref[c] = _dot_tn(_stack_values(gv_ref[0, sl, :]), _stack_heads(kd))
    for c in range(n_chunks):
        sl = chunk(c)
        a = jnp.where(causal, scores[c], 0.0).astype(BF16)
        v = gv_ref[0, sl, :]
        for hh in range(GLA_HEADS):
            vs = slice(hh * GLA_V_DIM, (hh + 1) * GLA_V_DIM)
            intra_ref[sl, vs] = _dot(a[hh * CHUNK:(hh + 1) * CHUNK], v[:, vs])

    st = st_ref[...]
    for c in range(n_chunks):
        sl = chunk(c)
        inter = _dot_nt(qm_ref[c], st.astype(BF16))
        r = gr_ref[0, sl, :].astype(F32)
        for hh in range(GLA_HEADS):
            vs = slice(hh * GLA_V_DIM, (hh + 1) * GLA_V_DIM)
            o = intra_ref[sl, vs] + inter[hh * CHUNK:(hh + 1) * CHUNK]
            o_ref[0, sl, vs] = (_rms(o, gnw) * _silu(r[:, vs])).astype(BF16)
        st = decay[c] * st + delta_ref[c]
    st_ref[...] = st


def _mlp_kernel(x_ref, oda_ref, ogla_ref, wout_ref, n2_ref, wup_ref, cw_ref, cb_ref, wdown_ref,
                fnw_ref, ut_ref, o_ref, carry_ref, ubuf_ref, act_ref):
    @pl.when(pl.program_id(1) == 0)
    def _():
        carry_ref[...] = ut_ref[...]

    t = x_ref.shape[1]
    mix = jnp.concatenate([oda_ref[0], ogla_ref[0]], axis=1)
    h1 = x_ref[0] + _dot(mix, wout_ref[...])
    o_ref[0] = h1
    xn = _rms(h1, n2_ref[...]).astype(BF16)

    def stage(col0, slot):
        cols = slice(col0, col0 + FF_CHUNK)
        u = _dot(xn, wup_ref[:, cols])
        ubuf_ref[slot, 0:SUB, :] = carry_ref[:, cols]
        ubuf_ref[slot, SUB:SUB + t, :] = u
        carry_ref[:, cols] = u[t - SUB:t]

    def conv(col0, slot):
        cols = slice(col0, col0 + FF_CHUNK)
        w = cw_ref[:, cols]
        return (w[0:1] * ubuf_ref[slot, SUB - 2:SUB - 2 + t, :] + w[1:2] * ubuf_ref[slot, SUB - 1:SUB - 1 + t, :]
                + w[2:3] * ubuf_ref[slot, SUB:SUB + t, :] + cb_ref[:, cols])

    def stage_chunk(j):
        stage(j * FF_CHUNK, (2 * j) % CONV_SLOTS)
        stage(D_FF + j * FF_CHUNK, (2 * j + 1) % CONV_SLOTS)

    n_chunks = D_FF // FF_CHUNK
    stage_chunk(0)
    for j in range(n_chunks):
        if j + 1 < n_chunks:
            stage_chunk(j + 1)
        val = conv(j * FF_CHUNK, (2 * j) % CONV_SLOTS)
        gate = conv(D_FF + j * FF_CHUNK, (2 * j + 1) % CONV_SLOTS)
        act_ref[:, j * FF_CHUNK:(j + 1) * FF_CHUNK] = (val * _silu(gate)).astype(BF16)
    o_ref[0] = _rms(o_ref[0] + _dot(act_ref[...], wdown_ref[...]), fnw_ref[...])


def _bf16_pieces(c):
    out = []
    for _ in range(3):
        p = np.float32(c).astype(BF16).astype(np.float32)
        out.append(p)
        c = np.float32(c) - p
    return out


@functools.lru_cache(maxsize=None)
def _attn_constants():
    assert KEY_TILE == DIAG_TILE == 512 and Q_SUPER <= 1024
    slopes = 2.0 ** (-8.0 * np.arange(1, DA_HEADS + 1, dtype=np.float64) / DA_HEADS)
    c0 = (slopes * LOG2E).astype(np.float32)
    augk = np.zeros((DA_HEADS, KEY_TILE, LANE), np.float32)
    augq = np.zeros((DA_HEADS, LANE, Q_SUPER), np.float32)
    j = np.arange(KEY_TILE)
    iq = np.arange(Q_SUPER)
    for h in range(DA_HEADS):
        a = _bf16_pieces(c0[h])
        for n in range(3):
            augk[h, :, n] = -a[n]
            augq[h, n, :] = iq % 256
            augk[h, :, 3 + n] = -a[n]
            augq[h, 3 + n, :] = (iq // 256) * 256
            augk[h, :, 6 + n] = j % 256
            augq[h, 6 + n, :] = a[n]
            augk[h, :, 9 + n] = (j // 256) * 256
            augq[h, 9 + n, :] = a[n]
    jj, ii = np.meshgrid(np.arange(KEY_TILE), np.arange(DIAG_TILE), indexing="ij")
    visible = (jj // CHUNK) <= (ii // CHUNK)
    bdiag = np.zeros((DA_HEADS, KEY_TILE, 2 * DIAG_TILE), np.float32)
    for h in range(DA_HEADS):
        corr = np.where(jj > ii, -2.0 * c0[h] * (jj - ii), 0.0)
        tile = np.where(visible, corr, -np.inf).astype(np.float32)
        bdiag[h] = np.concatenate([tile, tile], axis=1)
    return c0, augk, augq, bdiag


@functools.lru_cache(maxsize=None)
def _gla_constants():
    r = np.arange(CUMSUM_ROWS)
    return ((r[:, None] // CHUNK == r[None, :] // CHUNK) & (r[None, :] <= r[:, None])).astype(np.float32)


def _full(shape):
    return pl.BlockSpec(shape, lambda *_: (0,) * len(shape))


def _resident(shape):
    return pl.BlockSpec(shape, lambda *_: (0,) * len(shape), pipeline_mode=pl.Buffered(1))


def _params(sem, flags=None):
    return pltpu.CompilerParams(dimension_semantics=sem, vmem_limit_bytes=VMEM_LIMIT, flags=flags)


def kernel(x, meta_tokens, norm1_w, w_in, lambda_q1, lambda_k1, lambda_q2, lambda_k2, da_subln_w,
           gla_gate_w, gla_gate_b, gla_norm_w, w_out, norm2_w, w_up, conv_w, conv_b, w_down,
           final_norm_w):
    bsz, seq, _ = x.shape
    assert seq % Q_SUPER == 0 and seq % TOK_TILE == 0 and seq % GLA_TILE == 0 and seq % MLP_TILE == 0
    assert norm1_w.shape[0] == 1
    nq = seq // Q_SUPER

    w = w_in[0]
    wq, wk, wv = w[:, 0:512], w[:, 512:1024], w[:, 1024:1536]
    wrest, wlr = w[:, 1536:3072], w[:, 3072:3088]
    wnat = jnp.concatenate([wk, wrest, jnp.pad(wlr, ((0, 0), (0, LANE - GLA_GATE_RANK)))], axis=1).astype(BF16)
    wt = jnp.concatenate([wq, wv], axis=1).T.astype(BF16)
    gw = jnp.pad(gla_gate_w[0], ((0, LANE - GLA_GATE_RANK), (0, 0))).astype(BF16)
    gb = gla_gate_b[0][None].astype(F32)
    n1 = norm1_w[0][None]
    n2 = norm2_w[0][None]
    fnw = final_norm_w[None]
    gnw = gla_norm_w[0][None]
    subw_row = da_subln_w[0][None]
    subw_col = jnp.broadcast_to(da_subln_w[0][:, None], (DA_V_DIM, Q_SUPER))
    lam4 = jnp.concatenate([lambda_q1, lambda_k1, lambda_q2, lambda_k2], axis=0)
    wout = w_out[0].astype(BF16)
    wup = w_up[0].astype(BF16)
    wdown = w_down[0].astype(BF16)
    cw = conv_w[0]
    cb = conv_b[0][None]

    c0_np, augk_np, augq_np, bdiag_np = _attn_constants()
    tri_np = _gla_constants()
    c0 = jnp.asarray(c0_np)
    augk = jnp.asarray(augk_np, BF16)
    augq = jnp.asarray(augq_np, BF16)
    bdiag = jnp.asarray(bdiag_np)
    tri = jnp.asarray(tri_np, BF16)

    smem = pl.BlockSpec(memory_space=pltpu.SMEM)
    vmem = pl.BlockSpec(memory_space=pltpu.VMEM)

    k_meta, v_meta, st0, utail = pl.pallas_call(
        _meta_kernel,
        out_shape=(jax.ShapeDtypeStruct((N_META, 512), BF16),
                   jax.ShapeDtypeStruct((N_META, 512), BF16),
                   jax.ShapeDtypeStruct((GLA_V_DIM, GLA_KW), F32),
                   jax.ShapeDtypeStruct((8, 2 * D_FF), F32)),
        in_specs=[vmem, vmem, vmem, vmem, vmem, vmem, vmem, smem, vmem, vmem, vmem, vmem, vmem],
        out_specs=(vmem, vmem, vmem, vmem),
        compiler_params=pltpu.CompilerParams(vmem_limit_bytes=VMEM_LIMIT),
        name="meta_mixer",
    )(meta_tokens, n1, wnat, wt, gw, gb, lam4, c0, subw_row, gnw, wout, n2, wup)
    kmeta = k_meta.reshape(N_META, DA_HEADS, LANE).transpose(1, 0, 2)
    vtmeta = jnp.pad(v_meta.reshape(N_META, DA_HEADS, DA_V_DIM).transpose(1, 2, 0),
                     ((0, 0), (0, 0), (0, LANE - N_META)))

    nt = seq // TOK_TILE
    tok = lambda width: pl.BlockSpec((1, TOK_TILE, width), lambda b, t: (b, t, 0))
    tokt = pl.BlockSpec((1, 512, TOK_TILE), lambda b, t: (b, 0, t))
    kcat, qt, vt, gq, gk, gv, gr, la = pl.pallas_call(
        _proj_kernel,
        grid=(bsz, nt),
        out_shape=(jax.ShapeDtypeStruct((bsz, seq, 512), BF16),
                   jax.ShapeDtypeStruct((bsz, 512, seq), BF16),
                   jax.ShapeDtypeStruct((bsz, 512, seq), BF16),
                   jax.ShapeDtypeStruct((bsz, seq, GLA_KW), BF16),
                   jax.ShapeDtypeStruct((bsz, seq, GLA_KW), BF16),
                   jax.ShapeDtypeStruct((bsz, seq, GLA_WIDTH), BF16),
                   jax.ShapeDtypeStruct((bsz, seq, GLA_WIDTH), BF16),
                   jax.ShapeDtypeStruct((bsz, seq, GLA_KW), F32)),
        in_specs=[tok(D_MODEL), _full((1, D_MODEL)), _resident((D_MODEL, NAT_COLS)),
                  _resident((1024, D_MODEL)), _full((LANE, GLA_KW)), _full((1, GLA_KW))],
        out_specs=(tok(512), tokt, tokt, tok(GLA_KW), tok(GLA_KW), tok(GLA_WIDTH), tok(GLA_WIDTH),
                   tok(GLA_KW)),
        compiler_params=_params(("parallel", "arbitrary")),
        name="in_proj",
    )(x, n1, wnat, wt, gw, gb)

    per_head = lambda shape: pl.BlockSpec((1,) + shape, lambda b, h: (h, 0, 0))
    row = pltpu.VMEM((1, 2 * Q_SUPER), F32)
    o_da = pl.pallas_call(
        _attn_kernel,
        grid=(bsz, DA_HEADS),
        out_shape=jax.ShapeDtypeStruct((bsz, seq, DA_WIDTH), BF16),
        in_specs=[smem, _full((4, DA_QK_DIM)),
                  pl.BlockSpec((1, 2 * DA_QK_DIM, seq), lambda b, h: (b, h, 0)),
                  pl.BlockSpec((1, seq, LANE), lambda b, h: (b, 0, h)),
                  pl.BlockSpec((1, DA_V_DIM, seq), lambda b, h: (b, h, 0)),
                  per_head((KEY_TILE, LANE)), per_head((LANE, Q_SUPER)), per_head((KEY_TILE, 2 * DIAG_TILE)),
                  per_head((N_META, LANE)), per_head((DA_V_DIM, LANE)), _full((DA_V_DIM, Q_SUPER))],
        out_specs=pl.BlockSpec((1, seq, DA_V_DIM), lambda b, h: (b, 0, h)),
        scratch_shapes=[pltpu.VMEM((2 * LANE, 2 * Q_SUPER), BF16),
                        pltpu.VMEM((2 * LANE, 2 * Q_SUPER), BF16),
                        row, row, row, row,
                        pltpu.VMEM((DA_V_DIM, 2 * Q_SUPER), F32),
                        pltpu.VMEM((DA_V_DIM, 2 * Q_SUPER), F32),
                        pltpu.VMEM((KEY_TILE, 2 * Q_SUPER), F32),
                        pltpu.VMEM((KEY_TILE, 2 * Q_SUPER), F32),
                        row, row],
        compiler_params=_params(("parallel", "parallel")),
        name="diff_attn",
    )(c0, lam4, qt, kcat, vt, augk, augq, bdiag, kmeta, vtmeta, subw_col)

    ng = seq // GLA_TILE
    gtok = lambda width: pl.BlockSpec((1, GLA_TILE, width), lambda b, t: (b, t, 0))
    o_gla = pl.pallas_call(
        _gla_kernel,
        grid=(bsz, ng),
        out_shape=jax.ShapeDtypeStruct((bsz, seq, GLA_WIDTH), BF16),
        in_specs=[gtok(GLA_KW), gtok(GLA_KW), gtok(GLA_WIDTH), gtok(GLA_WIDTH), gtok(GLA_KW),
                  _full((CUMSUM_ROWS, CUMSUM_ROWS)), _full((GLA_V_DIM, GLA_KW)), _full((1, GLA_V_DIM))],
        out_specs=gtok(GLA_WIDTH),
        scratch_shapes=[pltpu.VMEM((GLA_V_DIM, GLA_KW), F32),
                        pltpu.VMEM((GLA_TILE // CHUNK, GLA_V_DIM, GLA_KW), F32),
                        pltpu.VMEM((GLA_TILE, GLA_WIDTH), F32),
                        pltpu.VMEM((GLA_TILE // CHUNK, GLA_HEADS * CHUNK, GLA_KW), BF16)],
        compiler_params=_params(("parallel", "arbitrary")),
        name="gla",
    )(gq, gk, gv, gr, la, tri, st0, gnw)

    mtok = lambda width: pl.BlockSpec((1, MLP_TILE, width), lambda b, t: (b, t, 0))
    out = pl.pallas_call(
        _mlp_kernel,
        grid=(bsz, seq // MLP_TILE),
        out_shape=jax.ShapeDtypeStruct((bsz, seq, D_MODEL), F32),
        in_specs=[mtok(D_MODEL), mtok(DA_WIDTH), mtok(GLA_WIDTH), _resident((D_MODEL, D_MODEL)),
                  _full((1, D_MODEL)), _resident((D_MODEL, 2 * D_FF)), _full((3, 2 * D_FF)),
                  _full((1, 2 * D_FF)), _resident((D_FF, D_MODEL)), _full((1, D_MODEL)),
                  _full((8, 2 * D_FF))],
        out_specs=mtok(D_MODEL),
        scratch_shapes=[pltpu.VMEM((8, 2 * D_FF), F32),
                        pltpu.VMEM((CONV_SLOTS, MLP_TILE + SUB, FF_CHUNK), F32),
                        pltpu.VMEM((MLP_TILE, D_FF), BF16)],
        compiler_params=_params(("parallel", "arbitrary")),
        name="out_proj_mlp",
    )(x, o_da, o_gla, wout, n2, wup, cw, cb, wdown, fnw, utail)
    return out
```

```python
import functools
import math

import numpy as np
import jax
import jax.numpy as jnp
from jax import lax
from jax.experimental import pallas as pl
from jax.experimental.pallas import tpu as pltpu

F32 = jnp.float32
BF16 = jnp.bfloat16

D_MODEL = 1024
N_META = 16
CHUNK = 64
Q_SUPER = 1024
DIAG_TILE = 512
DA_HEADS = 4
DA_QK_DIM = 64
DA_V_DIM = 128
DA_WIDTH = DA_HEADS * DA_V_DIM
GLA_HEADS = 4
GLA_K_DIM = 64
GLA_V_DIM = 128
GLA_KW = GLA_HEADS * GLA_K_DIM
GLA_WIDTH = GLA_HEADS * GLA_V_DIM
GLA_GATE_RANK = 16
GLA_GATE_TAU = 16.0
D_FF = 2816
RMS_EPS = 1e-6
LAM_INIT = 0.8 - 0.6 * math.exp(-0.3 * 0)

LOG2E = 1.4426950408889634
Q_SCALE = DA_QK_DIM ** -0.5 * LOG2E
GLA_Q_SCALE = GLA_K_DIM ** -0.5
LANE = 128
SUB = 8
KEY_TILE = 512
SCORE_STRIP = 256
GLA_TILE = 1024
CUMSUM_ROWS = 128
TOK_TILE = 1024
MLP_TILE = 512
FF_CHUNK = 256
CONV_SLOTS = 4
NAT_COLS = 2048 + LANE
VMEM_LIMIT = 56 * 1024 * 1024

_NT = (((1,), (1,)), ((), ()))
_TN = (((0,), (0,)), ((), ()))


def _dot(a, b):
    return jnp.dot(a, b, preferred_element_type=F32)


def _dot_nt(a, b):
    return lax.dot_general(a, b, _NT, preferred_element_type=F32)


def _dot_tn(a, b):
    return lax.dot_general(a, b, _TN, preferred_element_type=F32)


def _rms(x, w):
    return x * lax.rsqrt(jnp.mean(x * x, axis=-1, keepdims=True) + RMS_EPS) * w


def _log_sigmoid(z):
    return jnp.minimum(z, 0.0) - jnp.log(1.0 + jnp.exp(-jnp.abs(z)))


def _silu(g):
    hg = 0.5 * g
    return hg * jnp.tanh(hg) + hg


def _split3(x):
    hi = x.astype(BF16)
    r = x - hi.astype(F32)
    mid = r.astype(BF16)
    lo = (r - mid.astype(F32)).astype(BF16)
    return hi, mid, lo


def _cumsum_rows(tri, x):
    hi, mid, lo = _split3(x)
    return _dot(tri, hi) + _dot(tri, mid) + _dot(tri, lo)


def _lam(lam4):
    s1 = jnp.sum(lam4[0:1] * lam4[1:2], axis=-1, keepdims=True)
    s2 = jnp.sum(lam4[2:3] * lam4[3:4], axis=-1, keepdims=True)
    return jnp.exp(s1) - jnp.exp(s2) + LAM_INIT


def _head_mask(x, h, width):
    lane = lax.broadcasted_iota(jnp.int32, x.shape, 1)
    return jnp.where((lane >= h * width) & (lane < (h + 1) * width), x, jnp.zeros_like(x))


def _meta_kernel(meta_ref, n1_ref, wnat_ref, wt_ref, gw_ref, gb_ref, lam4_ref, c0_ref, subw_ref,
                 gnw_ref, wout_ref, n2_ref, wup_ref,
                 k_out, v_out, st_out, ut_out):
    m = N_META
    x = meta_ref[...]
    ub = _rms(x, n1_ref[...]).astype(BF16)
    nat = _dot(ub, wnat_ref[...])
    qv = _dot_nt(ub, wt_ref[...])
    kb = nat[:, 0:512].astype(BF16)
    qb = (qv[:, 0:512] * Q_SCALE).astype(BF16)
    vb = qv[:, 512:1024].astype(BF16)
    k_out[...] = kb
    v_out[...] = vb

    lam = _lam(lam4_ref[...])
    ii = lax.broadcasted_iota(jnp.int32, (m, m), 0)
    jj = lax.broadcasted_iota(jnp.int32, (m, m), 1)
    dist = jnp.abs(ii - jj).astype(F32)

    def softmax2(s):
        p = jnp.exp2(s - jnp.max(s, axis=-1, keepdims=True))
        return p / jnp.sum(p, axis=-1, keepdims=True)

    o_da = []
    for h in range(DA_HEADS):
        bias = -c0_ref[h] * dist
        p1 = softmax2(_dot_nt(_head_mask(qb, 2 * h, DA_QK_DIM), kb) + bias)
        p2 = softmax2(_dot_nt(_head_mask(qb, 2 * h + 1, DA_QK_DIM), kb) + bias)
        o = _dot((p1 - lam * p2).astype(BF16), vb[:, h * DA_V_DIM:(h + 1) * DA_V_DIM])
        o_da.append(_rms(o, subw_ref[...]) * (1.0 - LAM_INIT))
    o_da = jnp.concatenate(o_da, axis=1)

    gq = nat[:, 512:768].astype(BF16).astype(F32)
    gk = nat[:, 768:1024].astype(BF16).astype(F32)
    gv = nat[:, 1024:1536].astype(BF16)
    gr = nat[:, 1536:2048].astype(BF16).astype(F32)
    g16 = nat[:, 2048:NAT_COLS].astype(BF16)
    la = _log_sigmoid(_dot(g16, gw_ref[...]) + gb_ref[...]) / GLA_GATE_TAU
    tri = (jj <= ii).astype(BF16)
    b = _cumsum_rows(tri, la)
    bl = b[m - 1:m]
    qd = (gq * GLA_Q_SCALE * jnp.exp(b)).astype(BF16)
    ki = (gk * jnp.exp(-b)).astype(BF16)
    kd = (gk * jnp.exp(bl - b)).astype(BF16)
    o_gla = []
    for h in range(GLA_HEADS):
        a = _dot_nt(_head_mask(qd, h, GLA_K_DIM), ki)
        a = jnp.where(jj <= ii, a, 0.0).astype(BF16)
        o = _dot(a, gv[:, h * GLA_V_DIM:(h + 1) * GLA_V_DIM])
        o_gla.append(_rms(o, gnw_ref[...]))
    o_gla = jnp.concatenate(o_gla, axis=1) * _silu(gr)
    st_out[...] = _dot_tn(_stack_values(gv), _stack_heads(kd))

    mix = jnp.concatenate([o_da, o_gla], axis=1).astype(BF16)
    h1 = x + _dot(mix, wout_ref[...])
    u = _dot(_rms(h1, n2_ref[...]).astype(BF16), wup_ref[...])
    ut_out[...] = u[m - 8:m]


def _proj_kernel(x_ref, n1_ref, wnat_ref, wt_ref, gw_ref, gb_ref,
                 k_out, qt_out, vt_out, gq_out, gk_out, gv_out, gr_out, la_out):
    ub = _rms(x_ref[0], n1_ref[...]).astype(BF16)
    nat = _dot(ub, wnat_ref[...])
    k_out[0] = nat[:, 0:512].astype(BF16)
    gq_out[0] = nat[:, 512:768].astype(BF16)
    gk_out[0] = nat[:, 768:1024].astype(BF16)
    gv_out[0] = nat[:, 1024:1536].astype(BF16)
    gr_out[0] = nat[:, 1536:2048].astype(BF16)
    g16 = nat[:, 2048:NAT_COLS].astype(BF16)
    la_out[0] = _log_sigmoid(_dot(g16, gw_ref[...]) + gb_ref[...]) / GLA_GATE_TAU
    tt = _dot_nt(wt_ref[...], ub)
    qt_out[0] = (tt[0:512] * Q_SCALE).astype(BF16)
    vt_out[0] = tt[512:1024].astype(BF16)


def _attn_kernel(c0_ref, lam4_ref, qt_ref, k_ref, vt_ref, augk_ref, augq_ref, bdiag_ref,
                 kmeta_ref, vtmeta_ref, subw_ref, o_ref,
                 qrhs0_ref, qrhs1_ref, m0_ref, m1_ref, l0_ref, l1_ref, acc0_ref, acc1_ref,
                 sa_ref, sb_ref, mxa_ref, mxb_ref):
    c0 = c0_ref[pl.program_id(1)]
    n_super = k_ref.shape[1] // Q_SUPER
    assert n_super % 2 == 0
    states = ((qrhs0_ref, m0_ref, l0_ref, acc0_ref), (qrhs1_ref, m1_ref, l1_ref, acc1_ref))
    lam = _lam(lam4_ref[...])

    def softmax_pv(st, s, mx, vt, c, lanes):
        _, m_ref, l_ref, acc_ref = st
        m_old = m_ref[:, lanes]
        m_new = jnp.maximum(m_old, mx + c)
        alpha = jnp.exp2(m_old - m_new)
        p = jnp.exp2(s - (m_new - c))
        l_ref[:, lanes] = alpha * l_ref[:, lanes] + jnp.sum(p, axis=0, keepdims=True)
        pb = p.astype(BF16)
        if pb.shape[0] < vt.shape[1]:
            pb = jnp.concatenate([pb, jnp.zeros((vt.shape[1] - pb.shape[0], pb.shape[1]), BF16)], axis=0)
        acc_ref[:, lanes] = alpha * acc_ref[:, lanes] + _dot(vt, pb)
        m_ref[:, lanes] = m_new

    n_strips = 2 * Q_SUPER // SCORE_STRIP
    per_map = Q_SUPER // SCORE_STRIP
    strips = [slice(n * SCORE_STRIP, (n + 1) * SCORE_STRIP) for n in range(n_strips)]

    def query_block(n):
        return (n % per_map) * SCORE_STRIP // DIAG_TILE

    def key_lhs(j):
        k0 = pl.multiple_of(j * KEY_TILE, KEY_TILE)
        return jnp.concatenate([k_ref[0, pl.ds(k0, KEY_TILE), :], augk_ref[0]], axis=1)

    def score_strip(st, lhs, s_ref, mx_ref, n):
        s = _dot(lhs, st[0][:, strips[n]])
        s_ref[:, strips[n]] = s
        mx_ref[:, strips[n]] = jnp.max(s, axis=0, keepdims=True)

    def tile_step(st, q0, j, s_ref, mx_ref, diag_block, nxt=None):
        k0 = pl.multiple_of(j * KEY_TILE, KEY_TILE)
        vt = vt_ref[0, :, pl.ds(k0, KEY_TILE)]
        c = -c0 * (q0 - k0).astype(F32)
        todo = []
        if nxt is not None:
            jn, sn_ref, mxn_ref, todo = nxt
            todo = list(todo)
            lhs_n = key_lhs(jn)
        for n, lanes in enumerate(strips):
            if todo:
                score_strip(st, lhs_n, sn_ref, mxn_ref, todo.pop(0))
            if diag_block is None or query_block(n) > diag_block:
                softmax_pv(st, s_ref[:, lanes], mx_ref[:, lanes], vt, c, lanes)
            elif query_block(n) == diag_block:
                sub = (n % per_map) * SCORE_STRIP % DIAG_TILE
                rows = KEY_TILE // 2 if sub == 0 else KEY_TILE
                b0 = (n // per_map) * DIAG_TILE + sub
                s = s_ref[0:rows, lanes] + bdiag_ref[0, 0:rows, b0:b0 + SCORE_STRIP]
                softmax_pv(st, s, jnp.max(s, axis=0, keepdims=True), vt[:, 0:rows], c, lanes)
        assert not todo

    every = list(range(n_strips))
    n_diag = Q_SUPER // KEY_TILE
    assert n_diag == 2 and KEY_TILE == DIAG_TILE

    def begin(qs, st):
        qrhs_ref, m_ref, l_ref, acc_ref = st
        q0 = pl.multiple_of(qs * Q_SUPER, Q_SUPER)
        qt = qt_ref[0, :, pl.ds(q0, Q_SUPER)]
        zeros = jnp.zeros((DA_QK_DIM, Q_SUPER), BF16)
        qrhs_ref[0:64, 0:Q_SUPER] = qt[0:64]
        qrhs_ref[64:128, 0:Q_SUPER] = zeros
        qrhs_ref[0:64, Q_SUPER:2 * Q_SUPER] = zeros
        qrhs_ref[64:128, Q_SUPER:2 * Q_SUPER] = qt[64:128]
        qrhs_ref[128:256, 0:Q_SUPER] = augq_ref[0]
        qrhs_ref[128:256, Q_SUPER:2 * Q_SUPER] = augq_ref[0]
        m_ref[...] = jnp.full(m_ref.shape, -jnp.inf, F32)
        l_ref[...] = jnp.zeros(l_ref.shape, F32)
        acc_ref[...] = jnp.zeros(acc_ref.shape, F32)
        s_meta = _dot(jnp.concatenate([kmeta_ref[0], augk_ref[0, 0:N_META, :]], axis=1), qrhs_ref[...])
        lhs0 = key_lhs(0)
        c_meta = -c0 * (q0 + N_META).astype(F32)
        for n, lanes in enumerate(strips):
            score_strip(st, lhs0, sa_ref, mxa_ref, n)
            sm = s_meta[:, lanes]
            softmax_pv(st, sm, jnp.max(sm, axis=0, keepdims=True), vtmeta_ref[0], c_meta, lanes)

    def main(qs, st):
        q0 = pl.multiple_of(qs * Q_SUPER, Q_SUPER)
        n_past = qs * n_diag

        def pair(j):
            tile_step(st, q0, j, sa_ref, mxa_ref, None, (j + 1, sb_ref, mxb_ref, every))
            tile_step(st, q0, j + 1, sb_ref, mxb_ref, None, (j + 2, sa_ref, mxa_ref, every))

        def two_pairs(jj, carry):
            pair(4 * jj)
            pair(4 * jj + 2)
            return carry

        lax.fori_loop(0, qs // 2, two_pairs, 0)

        @pl.when(qs % 2 == 1)
        def _():
            pair(n_past - 2)

        later = [n for n in every if query_block(n) == 1]
        tile_step(st, q0, n_past, sa_ref, mxa_ref, 0, (n_past + 1, sb_ref, mxb_ref, later))
        tile_step(st, q0, n_past + 1, sb_ref, mxb_ref, 1)

    def finish(qs, st):
        _, _, l_ref, acc_ref = st
        q0 = pl.multiple_of(qs * Q_SUPER, Q_SUPER)
        inv = 1.0 / l_ref[...]
        acc = acc_ref[...]
        o = (acc[:, 0:Q_SUPER] * inv[:, 0:Q_SUPER]
             - lam * (acc[:, Q_SUPER:2 * Q_SUPER] * inv[:, Q_SUPER:2 * Q_SUPER]))
        y = o * lax.rsqrt(jnp.mean(o * o, axis=0, keepdims=True) + RMS_EPS) * subw_ref[...] * (1.0 - LAM_INIT)
        o_ref[0, pl.ds(q0, Q_SUPER), :] = y.T.astype(BF16)

    begin(0, states[0])

    def two_super_blocks(i, carry):
        qa = 2 * i
        main(qa, states[0])
        begin(qa + 1, states[1])
        finish(qa, states[0])
        main(qa + 1, states[1])
        begin(jnp.minimum(qa + 2, n_super - 1), states[0])
        finish(qa + 1, states[1])
        return carry

    lax.fori_loop(0, n_super // 2, two_super_blocks, 0)


def _stack_heads(x):
    return jnp.concatenate([_head_mask(x, hh, GLA_K_DIM) for hh in range(GLA_HEADS)], axis=0)


def _stack_values(v):
    return jnp.concatenate([v[:, hh * GLA_V_DIM:(hh + 1) * GLA_V_DIM] for hh in range(GLA_HEADS)], axis=0)


def _gla_kernel(gq_ref, gk_ref, gv_ref, gr_ref, la_ref, tri_ref, st0_ref, gnw_ref,
                o_ref, st_ref, delta_ref, intra_ref, qm_ref):
    @pl.when(pl.program_id(1) == 0)
    def _():
        st_ref[...] = st0_ref[...]

    tri = tri_ref[...]
    la = la_ref[0]
    b = jnp.concatenate([_cumsum_rows(tri, la[g:g + CUMSUM_ROWS]) for g in range(0, GLA_TILE, CUMSUM_ROWS)],
                        axis=0)
    ci = lax.broadcasted_iota(jnp.int32, (GLA_HEADS * CHUNK, CHUNK), 0) % CHUNK
    si = lax.broadcasted_iota(jnp.int32, (GLA_HEADS * CHUNK, CHUNK), 1)
    causal = si <= ci
    gnw = gnw_ref[...]

    n_chunks = GLA_TILE // CHUNK
    chunk = lambda c: slice(c * CHUNK, (c + 1) * CHUNK)

    decay, scores = [], []
    for c in range(n_chunks):
        sl = chunk(c)
        bc = b[sl]
        bl = bc[CHUNK - 1:CHUNK]
        q = gq_ref[0, sl, :].astype(F32)
        k = gk_ref[0, sl, :].astype(F32)
        qd = (q * GLA_Q_SCALE * jnp.exp(bc)).astype(BF16)
        ki = (k * jnp.exp(-bc)).astype(BF16)
        kd = (k * jnp.exp(bl - bc)).astype(BF16)
        decay.append(jnp.exp(bl))
        qm = _stack_heads(qd)
        qm_ref[c] = qm
        scores.append(_dot_nt(qm, ki))
        delta_ref[c] = _dot_tn(_stack_values(gv_ref[0, sl, :]), _stack_heads(kd))
    for c in range(n_chunks):
        sl = chunk(c)
        a = jnp.where(causal, scores[c], 0.0).astype(BF16)
        v = gv_ref[0, sl, :]
        for hh in range(GLA_HEADS):
            vs = slice(hh * GLA_V_DIM, (hh + 1) * GLA_V_DIM)
            intra_ref[sl, vs] = _dot(a[hh * CHUNK:(hh + 1) * CHUNK], v[:, vs])

    st = st_ref[...]
    for c in range(n_chunks):
        sl = chunk(c)
        inter = _dot_nt(qm_ref[c], st.astype(BF16))
        r = gr_ref[0, sl, :].astype(F32)
        for hh in range(GLA_HEADS):
            vs = slice(hh * GLA_V_DIM, (hh + 1) * GLA_V_DIM)
            o = intra_ref[sl, vs] + inter[hh * CHUNK:(hh + 1) * CHUNK]
            o_ref[0, sl, vs] = (_rms(o, gnw) * _silu(r[:, vs])).astype(BF16)
        st = decay[c] * st + delta_ref[c]
    st_ref[...] = st


def _mlp_kernel(x_ref, oda_ref, ogla_ref, wout_ref, n2_ref, wup_ref, cw_ref, cb_ref, wdown_ref,
                fnw_ref, ut_ref, o_ref, carry_ref, ubuf_ref, act_ref):
    @pl.when(pl.program_id(1) == 0)
    def _():
        carry_ref[...] = ut_ref[...]

    t = x_ref.shape[1]
    mix = jnp.concatenate([oda_ref[0], ogla_ref[0]], axis=1)
    h1 = x_ref[0] + _dot(mix, wout_ref[...])
    o_ref[0] = h1
    xn = _rms(h1, n2_ref[...]).astype(BF16)

    def stage(col0, slot):
        cols = slice(col0, col0 + FF_CHUNK)
        u = _dot(xn, wup_ref[:, cols])
        ubuf_ref[slot, 0:SUB, :] = carry_ref[:, cols]
        ubuf_ref[slot, SUB:SUB + t, :] = u
        carry_ref[:, cols] = u[t - SUB:t]

    def conv(col0, slot):
        cols = slice(col0, col0 + FF_CHUNK)
        w = cw_ref[:, cols]
        return (w[0:1] * ubuf_ref[slot, SUB - 2:SUB - 2 + t, :] + w[1:2] * ubuf_ref[slot, SUB - 1:SUB - 1 + t, :]
                + w[2:3] * ubuf_ref[slot, SUB:SUB + t, :] + cb_ref[:, cols])

    def stage_chunk(j):
        stage(j * FF_CHUNK, (2 * j) % CONV_SLOTS)
        stage(D_FF + j * FF_CHUNK, (2 * j + 1) % CONV_SLOTS)

    n_chunks = D_FF // FF_CHUNK
    stage_chunk(0)
    for j in range(n_chunks):
        if j + 1 < n_chunks:
            stage_chunk(j + 1)
        val = conv(j * FF_CHUNK, (2 * j) % CONV_SLOTS)
        gate = conv(D_FF + j * FF_CHUNK, (2 * j + 1) % CONV_SLOTS)
        act_ref[:, j * FF_CHUNK:(j + 1) * FF_CHUNK] = (val * _silu(gate)).astype(BF16)
    o_ref[0] = _rms(o_ref[0] + _dot(act_ref[...], wdown_ref[...]), fnw_ref[...])


def _bf16_pieces(c):
    out = []
    for _ in range(3):
        p = np.float32(c).astype(BF16).astype(np.float32)
        out.append(p)
        c = np.float32(c) - p
    return out


@functools.lru_cache(maxsize=None)
def _attn_constants():
    assert KEY_TILE == DIAG_TILE == 512 and Q_SUPER <= 1024
    slopes = 2.0 ** (-8.0 * np.arange(1, DA_HEADS + 1, dtype=np.float64) / DA_HEADS)
    c0 = (slopes * LOG2E).astype(np.float32)
    augk = np.zeros((DA_HEADS, KEY_TILE, LANE), np.float32)
    augq = np.zeros((DA_HEADS, LANE, Q_SUPER), np.float32)
    j = np.arange(KEY_TILE)
    iq = np.arange(Q_SUPER)
    for h in range(DA_HEADS):
        a = _bf16_pieces(c0[h])
        for n in range(3):
            augk[h, :, n] = -a[n]
            augq[h, n, :] = iq % 256
            augk[h, :, 3 + n] = -a[n]
            augq[h, 3 + n, :] = (iq // 256) * 256
            augk[h, :, 6 + n] = j % 256
            augq[h, 6 + n, :] = a[n]
            augk[h, :, 9 + n] = (j // 256) * 256
            augq[h, 9 + n, :] = a[n]
    jj, ii = np.meshgrid(np.arange(KEY_TILE), np.arange(DIAG_TILE), indexing="ij")
    visible = (jj // CHUNK) <= (ii // CHUNK)
    bdiag = np.zeros((DA_HEADS, KEY_TILE, 2 * DIAG_TILE), np.float32)
    for h in range(DA_HEADS):
        corr = np.where(jj > ii, -2.0 * c0[h] * (jj - ii), 0.0)
        tile = np.where(visible, corr, -np.inf).astype(np.float32)
        bdiag[h] = np.concatenate([tile, tile], axis=1)
    return c0, augk, augq, bdiag


@functools.lru_cache(maxsize=None)
def _gla_constants():
    r = np.arange(CUMSUM_ROWS)
    return ((r[:, None] // CHUNK == r[None, :] // CHUNK) & (r[None, :] <= r[:, None])).astype(np.float32)


def _full(shape):
    return pl.BlockSpec(shape, lambda *_: (0,) * len(shape))


def _resident(shape):
    return pl.BlockSpec(shape, lambda *_: (0,) * len(shape), pipeline_mode=pl.Buffered(1))


def _params(sem, flags=None):
    return pltpu.CompilerParams(dimension_semantics=sem, vmem_limit_bytes=VMEM_LIMIT, flags=flags)


def kernel(x, meta_tokens, norm1_w, w_in, lambda_q1, lambda_k1, lambda_q2, lambda_k2, da_subln_w,
           gla_gate_w, gla_gate_b, gla_norm_w, w_out, norm2_w, w_up, conv_w, conv_b, w_down,
           final_norm_w):
    bsz, seq, _ = x.shape
    assert seq % Q_SUPER == 0 and seq % TOK_TILE == 0 and seq % GLA_TILE == 0 and seq % MLP_TILE == 0
    assert norm1_w.shape[0] == 1

    w = w_in[0]
    wq, wk, wv = w[:, 0:512], w[:, 512:1024], w[:, 1024:1536]
    wrest, wlr = w[:, 1536:3072], w[:, 3072:3088]
    wnat = jnp.concatenate([wk, wrest, jnp.pad(wlr, ((0, 0), (0, LANE - GLA_GATE_RANK)))], axis=1).astype(BF16)
    wt = jnp.concatenate([wq, wv], axis=1).T.astype(BF16)
    gw = jnp.pad(gla_gate_w[0], ((0, LANE - GLA_GATE_RANK), (0, 0))).astype(BF16)
    gb = gla_gate_b[0][None].astype(F32)
    n1 = norm1_w[0][None]
    n2 = norm2_w[0][None]
    fnw = final_norm_w[None]
    gnw = gla_norm_w[0][None]
    subw_row = da_subln_w[0][None]
    subw_col = jnp.broadcast_to(da_subln_w[0][:, None], (DA_V_DIM, Q_SUPER))
    lam4 = jnp.concatenate([lambda_q1, lambda_k1, lambda_q2, lambda_k2], axis=0)
    wout = w_out[0].astype(BF16)
    wup = w_up[0].astype(BF16)
    wdown = w_down[0].astype(BF16)
    cw = conv_w[0]
    cb = conv_b[0][None]

    c0_np, augk_np, augq_np, bdiag_np = _attn_constants()
    tri_np = _gla_constants()
    c0 = jnp.asarray(c0_np)
    augk = jnp.asarray(augk_np, BF16)
    augq = jnp.asarray(augq_np, BF16)
    bdiag = jnp.asarray(bdiag_np)
    tri = jnp.asarray(tri_np, BF16)

    smem = pl.BlockSpec(memory_space=pltpu.SMEM)
    vmem = pl.BlockSpec(memory_space=pltpu.VMEM)

    k_meta, v_meta, st0, utail = pl.pallas_call(
        _meta_kernel,
        out_shape=(jax.ShapeDtypeStruct((N_META, 512), BF16),
                   jax.ShapeDtypeStruct((N_META, 512), BF16),
                   jax.ShapeDtypeStruct((GLA_V_DIM, GLA_KW), F32),
                   jax.ShapeDtypeStruct((8, 2 * D_FF), F32)),
        in_specs=[vmem, vmem, vmem, vmem, vmem, vmem, vmem, smem, vmem, vmem, vmem, vmem, vmem],
        out_specs=(vmem, vmem, vmem, vmem),
        compiler_params=pltpu.CompilerParams(vmem_limit_bytes=VMEM_LIMIT),
        name="meta_mixer",
    )(meta_tokens, n1, wnat, wt, gw, gb, lam4, c0, subw_row, gnw, wout, n2, wup)
    kmeta = k_meta.reshape(N_META, DA_HEADS, LANE).transpose(1, 0, 2)
    vtmeta = jnp.pad(v_meta.reshape(N_META, DA_HEADS, DA_V_DIM).transpose(1, 2, 0),
                     ((0, 0), (0, 0), (0, LANE - N_META)))

    nt = seq // TOK_TILE
    tok = lambda width: pl.BlockSpec((1, TOK_TILE, width), lambda b, t: (b, t, 0))
    tokt = pl.BlockSpec((1, 512, TOK_TILE), lambda b, t: (b, 0, t))
    kcat, qt, vt, gq, gk, gv, gr, la = pl.pallas_call(
        _proj_kernel,
        grid=(bsz, nt),
        out_shape=(jax.ShapeDtypeStruct((bsz, seq, 512), BF16),
                   jax.ShapeDtypeStruct((bsz, 512, seq), BF16),
                   jax.ShapeDtypeStruct((bsz, 512, seq), BF16),
                   jax.ShapeDtypeStruct((bsz, seq, GLA_KW), BF16),
                   jax.ShapeDtypeStruct((bsz, seq, GLA_KW), BF16),
                   jax.ShapeDtypeStruct((bsz, seq, GLA_WIDTH), BF16),
                   jax.ShapeDtypeStruct((bsz, seq, GLA_WIDTH), BF16),
                   jax.ShapeDtypeStruct((bsz, seq, GLA_KW), F32)),
        in_specs=[tok(D_MODEL), _full((1, D_MODEL)), _resident((D_MODEL, NAT_COLS)),
                  _resident((1024, D_MODEL)), _full((LANE, GLA_KW)), _full((1, GLA_KW))],
        out_specs=(tok(512), tokt, tokt, tok(GLA_KW), tok(GLA_KW), tok(GLA_WIDTH), tok(GLA_WIDTH),
                   tok(GLA_KW)),
        compiler_params=_params(("parallel", "arbitrary")),
        name="in_proj",
    )(x, n1, wnat, wt, gw, gb)

    per_head = lambda shape: pl.BlockSpec((1,) + shape, lambda b, h: (h, 0, 0))
    row = pltpu.VMEM((1, 2 * Q_SUPER), F32)
    o_da = pl.pallas_call(
        _attn_kernel,
        grid=(bsz, DA_HEADS),
        out_shape=jax.ShapeDtypeStruct((bsz, seq, DA_WIDTH), BF16),
        in_specs=[smem, _full((4, DA_QK_DIM)),
                  pl.BlockSpec((1, 2 * DA_QK_DIM, seq), lambda b, h: (b, h, 0)),
                  pl.BlockSpec((1, seq, LANE), lambda b, h: (b, 0, h)),
                  pl.BlockSpec((1, DA_V_DIM, seq), lambda b, h: (b, h, 0)),
                  per_head((KEY_TILE, LANE)), per_head((LANE, Q_SUPER)), per_head((KEY_TILE, 2 * DIAG_TILE)),
                  per_head((N_META, LANE)), per_head((DA_V_DIM, LANE)), _full((DA_V_DIM, Q_SUPER))],
        out_specs=pl.BlockSpec((1, seq, DA_V_DIM), lambda b, h: (b, 0, h)),
        scratch_shapes=[pltpu.VMEM((2 * LANE, 2 * Q_SUPER), BF16),
                        pltpu.VMEM((2 * LANE, 2 * Q_SUPER), BF16),
                        row, row, row, row,
                        pltpu.VMEM((DA_V_DIM, 2 * Q_SUPER), F32),
                        pltpu.VMEM((DA_V_DIM, 2 * Q_SUPER), F32),
                        pltpu.VMEM((KEY_TILE, 2 * Q_SUPER), F32),
                        pltpu.VMEM((KEY_TILE, 2 * Q_SUPER), F32),
                        row, row],
        compiler_params=_params(("parallel", "parallel")),
        name="diff_attn",
    )(c0, lam4, qt, kcat, vt, augk, augq, bdiag, kmeta, vtmeta, subw_col)

    ng = seq // GLA_TILE
    gtok = lambda width: pl.BlockSpec((1, GLA_TILE, width), lambda b, t: (b, t, 0))
    o_gla = pl.pallas_call(
        _gla_kernel,
        grid=(bsz, ng),
        out_shape=jax.ShapeDtypeStruct((bsz, seq, GLA_WIDTH), BF16),
        in_specs=[gtok(GLA_KW), gtok(GLA_KW), gtok(GLA_WIDTH), gtok(GLA_WIDTH), gtok(GLA_KW),
                  _full((CUMSUM_ROWS, CUMSUM_ROWS)), _full((GLA_V_DIM, GLA_KW)), _full((1, GLA_V_DIM))],
        out_specs=gtok(GLA_WIDTH),
        scratch_shapes=[pltpu.VMEM((GLA_V_DIM, GLA_KW), F32),
                        pltpu.VMEM((GLA_TILE // CHUNK, GLA_V_DIM, GLA_KW), F32),
                        pltpu.VMEM((GLA_TILE, GLA_WIDTH), F32),
                        pltpu.VMEM((GLA_TILE // CHUNK, GLA_HEADS * CHUNK, GLA_KW), BF16)],
        compiler_params=_params(("parallel", "arbitrary")),
        name="gla",
    )(gq, gk, gv, gr, la, tri, st0, gnw)

    mtok = lambda width: pl.BlockSpec((1, MLP_TILE, width), lambda b, t: (b, t, 0))
    out = pl.pallas_call(
        _mlp_kernel,
        grid=(bsz, seq // MLP_TILE),
        out_shape=jax.ShapeDtypeStruct((bsz, seq, D_MODEL), F32),
        in_specs=[mtok(D_MODEL), mtok(DA_WIDTH), mtok(GLA_WIDTH), _resident((D_MODEL, D_MODEL)),
                  _full((1, D_MODEL)), _resident((D_MODEL, 2 * D_FF)), _full((3, 2 * D_FF)),
                  _full((1, 2 * D_FF)), _resident((D_FF, D_MODEL)), _full((1, D_MODEL)),
                  _full((8, 2 * D_FF))],
        out_specs=mtok(D_MODEL),
        scratch_shapes=[pltpu.VMEM((8, 2 * D_FF), F32),
                        pltpu.VMEM((CONV_SLOTS, MLP_TILE + SUB, FF_CHUNK), F32),
                        pltpu.VMEM((MLP_TILE, D_FF), BF16)],
        compiler_params=_params(("parallel", "arbitrary")),
        name="out_proj_mlp",
    )(x, o_da, o_gla, wout, n2, wup, cw, cb, wdown, fnw, utail)
    return out
```

```python
import functools
import math

import numpy as np
import jax
import jax.numpy as jnp
from jax import lax
from jax.experimental import pallas as pl
from jax.experimental.pallas import tpu as pltpu

F32 = jnp.float32
BF16 = jnp.bfloat16

D_MODEL = 1024
N_META = 16
CHUNK = 64
Q_SUPER = 1024
DIAG_TILE = 512
DA_HEADS = 4
DA_QK_DIM = 64
DA_V_DIM = 128
DA_WIDTH = DA_HEADS * DA_V_DIM
GLA_HEADS = 4
GLA_K_DIM = 64
GLA_V_DIM = 128
GLA_KW = GLA_HEADS * GLA_K_DIM
GLA_WIDTH = GLA_HEADS * GLA_V_DIM
GLA_GATE_RANK = 16
GLA_GATE_TAU = 16.0
D_FF = 2816
RMS_EPS = 1e-6
LAM_INIT = 0.8 - 0.6 * math.exp(-0.3 * 0)

LOG2E = 1.4426950408889634
Q_SCALE = DA_QK_DIM ** -0.5 * LOG2E
GLA_Q_SCALE = GLA_K_DIM ** -0.5
LANE = 128
SUB = 8
KEY_TILE = 512
SCORE_STRIP = 256
GLA_TILE = 1024
CUMSUM_ROWS = 128
TOK_TILE = 1024
MLP_TILE = 512
FF_CHUNK = 256
CONV_SLOTS = 4
NAT_COLS = 2048 + LANE
VMEM_LIMIT = 56 * 1024 * 1024

_NT = (((1,), (1,)), ((), ()))
_TN = (((0,), (0,)), ((), ()))


def _dot(a, b):
    return jnp.dot(a, b, preferred_element_type=F32)


def _dot_nt(a, b):
    return lax.dot_general(a, b, _NT, preferred_element_type=F32)


def _dot_tn(a, b):
    return lax.dot_general(a, b, _TN, preferred_element_type=F32)


def _rms(x, w):
    return x * lax.rsqrt(jnp.mean(x * x, axis=-1, keepdims=True) + RMS_EPS) * w


def _log_sigmoid(z):
    return jnp.minimum(z, 0.0) - jnp.log(1.0 + jnp.exp(-jnp.abs(z)))


def _silu(g):
    hg = 0.5 * g
    return hg * jnp.tanh(hg) + hg


def _split3(x):
    hi = x.astype(BF16)
    r = x - hi.astype(F32)
    mid = r.astype(BF16)
    lo = (r - mid.astype(F32)).astype(BF16)
    return hi, mid, lo


def _cumsum_rows(tri, x):
    hi, mid, lo = _split3(x)
    return _dot(tri, hi) + _dot(tri, mid) + _dot(tri, lo)


def _lam(lam4):
    s1 = jnp.sum(lam4[0:1] * lam4[1:2], axis=-1, keepdims=True)
    s2 = jnp.sum(lam4[2:3] * lam4[3:4], axis=-1, keepdims=True)
    return jnp.exp(s1) - jnp.exp(s2) + LAM_INIT


def _head_mask(x, h, width):
    lane = lax.broadcasted_iota(jnp.int32, x.shape, 1)
    return jnp.where((lane >= h * width) & (lane < (h + 1) * width), x, jnp.zeros_like(x))


def _meta_kernel(meta_ref, n1_ref, wnat_ref, wt_ref, gw_ref, gb_ref, lam4_ref, c0_ref, subw_ref,
                 gnw_ref, wout_ref, n2_ref, wup_ref,
                 k_out, v_out, st_out, ut_out):
    m = N_META
    x = meta_ref[...]
    ub = _rms(x, n1_ref[...]).astype(BF16)
    nat = _dot(ub, wnat_ref[...])
    qv = _dot_nt(ub, wt_ref[...])
    kb = nat[:, 0:512].astype(BF16)
    qb = (qv[:, 0:512] * Q_SCALE).astype(BF16)
    vb = qv[:, 512:1024].astype(BF16)
    k_out[...] = kb
    v_out[...] = vb

    lam = _lam(lam4_ref[...])
    ii = lax.broadcasted_iota(jnp.int32, (m, m), 0)
    jj = lax.broadcasted_iota(jnp.int32, (m, m), 1)
    dist = jnp.abs(ii - jj).astype(F32)

    def softmax2(s):
        p = jnp.exp2(s - jnp.max(s, axis=-1, keepdims=True))
        return p / jnp.sum(p, axis=-1, keepdims=True)

    o_da = []
    for h in range(DA_HEADS):
        bias = -c0_ref[h] * dist
        p1 = softmax2(_dot_nt(_head_mask(qb, 2 * h, DA_QK_DIM), kb) + bias)
        p2 = softmax2(_dot_nt(_head_mask(qb, 2 * h + 1, DA_QK_DIM), kb) + bias)
        o = _dot((p1 - lam * p2).astype(BF16), vb[:, h * DA_V_DIM:(h + 1) * DA_V_DIM])
        o_da.append(_rms(o, subw_ref[...]) * (1.0 - LAM_INIT))
    o_da = jnp.concatenate(o_da, axis=1)

    gq = nat[:, 512:768].astype(BF16).astype(F32)
    gk = nat[:, 768:1024].astype(BF16).astype(F32)
    gv = nat[:, 1024:1536].astype(BF16)
    gr = nat[:, 1536:2048].astype(BF16).astype(F32)
    g16 = nat[:, 2048:NAT_COLS].astype(BF16)
    la = _log_sigmoid(_dot(g16, gw_ref[...]) + gb_ref[...]) / GLA_GATE_TAU
    tri = (jj <= ii).astype(BF16)
    b = _cumsum_rows(tri, la)
    bl = b[m - 1:m]
    qd = (gq * GLA_Q_SCALE * jnp.exp(b)).astype(BF16)
    ki = (gk * jnp.exp(-b)).astype(BF16)
    kd = (gk * jnp.exp(bl - b)).astype(BF16)
    o_gla = []
    for h in range(GLA_HEADS):
        a = _dot_nt(_head_mask(qd, h, GLA_K_DIM), ki)
        a = jnp.where(jj <= ii, a, 0.0).astype(BF16)
        o = _dot(a, gv[:, h * GLA_V_DIM:(h + 1) * GLA_V_DIM])
        o_gla.append(_rms(o, gnw_ref[...]))
    o_gla = jnp.concatenate(o_gla, axis=1) * _silu(gr)
    st_out[...] = _dot_tn(_stack_values(gv), _stack_heads(kd))

    mix = jnp.concatenate([o_da, o_gla], axis=1).astype(BF16)
    h1 = x + _dot(mix, wout_ref[...])
    u = _dot(_rms(h1, n2_ref[...]).astype(BF16), wup_ref[...])
    ut_out[...] = u[m - 8:m]


def _proj_kernel(x_ref, n1_ref, wnat_ref, wt_ref, gw_ref, gb_ref,
                 k_out, qt_out, vt_out, gq_out, gk_out, gv_out, gr_out, la_out):
    ub = _rms(x_ref[0], n1_ref[...]).astype(BF16)
    nat = _dot(ub, wnat_ref[...])
    k_out[0] = nat[:, 0:512].astype(BF16)
    gq_out[0] = nat[:, 512:768].astype(BF16)
    gk_out[0] = nat[:, 768:1024].astype(BF16)
    gv_out[0] = nat[:, 1024:1536].astype(BF16)
    gr_out[0] = nat[:, 1536:2048].astype(BF16)
    g16 = nat[:, 2048:NAT_COLS].astype(BF16)
    la_out[0] = _log_sigmoid(_dot(g16, gw_ref[...]) + gb_ref[...]) / GLA_GATE_TAU
    tt = _dot_nt(wt_ref[...], ub)
    qt_out[0] = (tt[0:512] * Q_SCALE).astype(BF16)
    vt_out[0] = tt[512:1024].astype(BF16)


def _attn_kernel(c0_ref, lam4_ref, qt_ref, k_ref, vt_ref, augk_ref, augq_ref, bdiag_ref,
                 kmeta_ref, vtmeta_ref, subw_ref, o_ref,
                 qrhs0_ref, qrhs1_ref, m0_ref, m1_ref, l0_ref, l1_ref, acc0_ref, acc1_ref,
                 sa_ref, sb_ref, mxa_ref, mxb_ref):
    c0 = c0_ref[pl.program_id(1)]
    n_super = k_ref.shape[1] // Q_SUPER
    assert n_super % 2 == 0
    states = ((qrhs0_ref, m0_ref, l0_ref, acc0_ref), (qrhs1_ref, m1_ref, l1_ref, acc1_ref))
    lam = _lam(lam4_ref[...])

    def softmax_pv(st, s, mx, vt, c, lanes):
        _, m_ref, l_ref, acc_ref = st
        m_old = m_ref[:, lanes]
        m_new = jnp.maximum(m_old, mx + c)
        alpha = jnp.exp2(m_old - m_new)
        p = jnp.exp2(s - (m_new - c))
        l_ref[:, lanes] = alpha * l_ref[:, lanes] + jnp.sum(p, axis=0, keepdims=True)
        pb = p.astype(BF16)
        if pb.shape[0] < vt.shape[1]:
            pb = jnp.concatenate([pb, jnp.zeros((vt.shape[1] - pb.shape[0], pb.shape[1]), BF16)], axis=0)
        acc_ref[:, lanes] = alpha * acc_ref[:, lanes] + _dot(vt, pb)
        m_ref[:, lanes] = m_new

    n_strips = 2 * Q_SUPER // SCORE_STRIP
    per_map = Q_SUPER // SCORE_STRIP
    strips = [slice(n * SCORE_STRIP, (n + 1) * SCORE_STRIP) for n in range(n_strips)]

    def query_block(n):
        return (n % per_map) * SCORE_STRIP // DIAG_TILE

    def key_lhs(j):
        k0 = pl.multiple_of(j * KEY_TILE, KEY_TILE)
        return jnp.concatenate([k_ref[0, pl.ds(k0, KEY_TILE), :], augk_ref[0]], axis=1)

    def score_strip(st, lhs, s_ref, mx_ref, n):
        s = _dot(lhs, st[0][:, strips[n]])
        s_ref[:, strips[n]] = s
        mx_ref[:, strips[n]] = jnp.max(s, axis=0, keepdims=True)

    def tile_step(st, q0, j, s_ref, mx_ref, diag_block, nxt=None):
        k0 = pl.multiple_of(j * KEY_TILE, KEY_TILE)
        vt = vt_ref[0, :, pl.ds(k0, KEY_TILE)]
        c = -c0 * (q0 - k0).astype(F32)
        todo = []
        if nxt is not None:
            jn, sn_ref, mxn_ref, todo = nxt
            todo = list(todo)
            lhs_n = key_lhs(jn)
        for n, lanes in enumerate(strips):
            if todo:
                score_strip(st, lhs_n, sn_ref, mxn_ref, todo.pop(0))
            if diag_block is None or query_block(n) > diag_block:
                softmax_pv(st, s_ref[:, lanes], mx_ref[:, lanes], vt, c, lanes)
            elif query_block(n) == diag_block:
                sub = (n % per_map) * SCORE_STRIP % DIAG_TILE
                rows = KEY_TILE // 2 if sub == 0 else KEY_TILE
                b0 = (n // per_map) * DIAG_TILE + sub
                s = s_ref[0:rows, lanes] + bdiag_ref[0, 0:rows, b0:b0 + SCORE_STRIP]
                softmax_pv(st, s, jnp.max(s, axis=0, keepdims=True), vt[:, 0:rows], c, lanes)
        assert not todo

    every = list(range(n_strips))
    n_diag = Q_SUPER // KEY_TILE
    assert n_diag == 2 and KEY_TILE == DIAG_TILE

    def begin(qs, st):
        qrhs_ref, m_ref, l_ref, acc_ref = st
        q0 = pl.multiple_of(qs * Q_SUPER, Q_SUPER)
        qt = qt_ref[0, :, pl.ds(q0, Q_SUPER)]
        zeros = jnp.zeros((DA_QK_DIM, Q_SUPER), BF16)
        qrhs_ref[0:64, 0:Q_SUPER] = qt[0:64]
        qrhs_ref[64:128, 0:Q_SUPER] = zeros
        qrhs_ref[0:64, Q_SUPER:2 * Q_SUPER] = zeros
        qrhs_ref[64:128, Q_SUPER:2 * Q_SUPER] = qt[64:128]
        qrhs_ref[128:256, 0:Q_SUPER] = augq_ref[0]
        qrhs_ref[128:256, Q_SUPER:2 * Q_SUPER] = augq_ref[0]
        m_ref[...] = jnp.full(m_ref.shape, -jnp.inf, F32)
        l_ref[...] = jnp.zeros(l_ref.shape, F32)
        acc_ref[...] = jnp.zeros(acc_ref.shape, F32)
        s_meta = _dot(jnp.concatenate([kmeta_ref[0], augk_ref[0, 0:N_META, :]], axis=1), qrhs_ref[...])
        lhs0 = key_lhs(0)
        c_meta = -c0 * (q0 + N_META).astype(F32)
        for n, lanes in enumerate(strips):
            score_strip(st, lhs0, sa_ref, mxa_ref, n)
            sm = s_meta[:, lanes]
            softmax_pv(st, sm, jnp.max(sm, axis=0, keepdims=True), vtmeta_ref[0], c_meta, lanes)

    def main(qs, st):
        q0 = pl.multiple_of(qs * Q_SUPER, Q_SUPER)
        n_past = qs * n_diag

        def pair(j):
            tile_step(st, q0, j, sa_ref, mxa_ref, None, (j + 1, sb_ref, mxb_ref, every))
            tile_step(st, q0, j + 1, sb_ref, mxb_ref, None, (j + 2, sa_ref, mxa_ref, every))

        def two_pairs(jj, carry):
            pair(4 * jj)
            pair(4 * jj + 2)
            return carry

        lax.fori_loop(0, qs // 2, two_pairs, 0)

        @pl.when(qs % 2 == 1)
        def _():
            pair(n_past - 2)

        later = [n for n in every if query_block(n) == 1]
        tile_step(st, q0, n_past, sa_ref, mxa_ref, 0, (n_past + 1, sb_ref, mxb_ref, later))
        tile_step(st, q0, n_past + 1, sb_ref, mxb_ref, 1)

    def finish(qs, st):
        _, _, l_ref, acc_ref = st
        q0 = pl.multiple_of(qs * Q_SUPER, Q_SUPER)
        inv = 1.0 / l_ref[...]
        acc = acc_ref[...]
        o = (acc[:, 0:Q_SUPER] * inv[:, 0:Q_SUPER]
             - lam * (acc[:, Q_SUPER:2 * Q_SUPER] * inv[:, Q_SUPER:2 * Q_SUPER]))
        y = o * lax.rsqrt(jnp.mean(o * o, axis=0, keepdims=True) + RMS_EPS) * subw_ref[...] * (1.0 - LAM_INIT)
        o_ref[0, pl.ds(q0, Q_SUPER), :] = y.T.astype(BF16)

    begin(0, states[0])

    def two_super_blocks(i, carry):
        qa = 2 * i
        main(qa, states[0])
        begin(qa + 1, states[1])
        finish(qa, states[0])
        main(qa + 1, states[1])
        begin(jnp.minimum(qa + 2, n_super - 1), states[0])
        finish(qa + 1, states[1])
        return carry

    lax.fori_loop(0, n_super // 2, two_super_blocks, 0)


def _stack_heads(x):
    return jnp.concatenate([_head_mask(x, hh, GLA_K_DIM) for hh in range(GLA_HEADS)], axis=0)


def _stack_values(v):
    return jnp.concatenate([v[:, hh * GLA_V_DIM:(hh + 1) * GLA_V_DIM] for hh in range(GLA_HEADS)], axis=0)


def _gla_kernel(gq_ref, gk_ref, gv_ref, gr_ref, la_ref, tri_ref, st0_ref, gnw_ref,
                o_ref, st_ref, delta_ref, intra_ref, qm_ref):
    @pl.when(pl.program_id(1) == 0)
    def _():
        st_ref[...] = st0_ref[...]

    tri = tri_ref[...]
    la = la_ref[0]
    b = jnp.concatenate([_cumsum_rows(tri, la[g:g + CUMSUM_ROWS]) for g in range(0, GLA_TILE, CUMSUM_ROWS)],
                        axis=0)
    ci = lax.broadcasted_iota(jnp.int32, (GLA_HEADS * CHUNK, CHUNK), 0) % CHUNK
    si = lax.broadcasted_iota(jnp.int32, (GLA_HEADS * CHUNK, CHUNK), 1)
    causal = si <= ci
    gnw = gnw_ref[...]

    n_chunks = GLA_TILE // CHUNK
    chunk = lambda c: slice(c * CHUNK, (c + 1) * CHUNK)

    decay, scores = [], []
    for c in range(n_chunks):
        sl = chunk(c)
        bc = b[sl]
        bl = bc[CHUNK - 1:CHUNK]
        q = gq_ref[0, sl, :].astype(F32)
        k = gk_ref[0, sl, :].astype(F32)
        qd = (q * GLA_Q_SCALE * jnp.exp(bc)).astype(BF16)
        ki = (k * jnp.exp(-bc)).astype(BF16)
        kd = (k * jnp.exp(bl - bc)).astype(BF16)
        decay.append(jnp.exp(bl))
        qm = _stack_heads(qd)
        qm_ref[c] = qm
        scores.append(_dot_nt(qm, ki))
        delta_ref[c] = _dot_tn(_stack_values(gv_ref[0, sl, :]), _stack_heads(kd))
    for c in range(n_chunks):
        sl = chunk(c)
        a = jnp.where(causal, scores[c], 0.0).astype(BF16)
        v = gv_ref[0, sl, :]
        for hh in range(GLA_HEADS):
            vs = slice(hh * GLA_V_DIM, (hh + 1) * GLA_V_DIM)
            intra_ref[sl, vs] = _dot(a[hh * CHUNK:(hh + 1) * CHUNK], v[:, vs])

    st = st_ref[...]
    for c in range(n_chunks):
        sl = chunk(c)
        inter = _dot_nt(qm_ref[c], st.astype(BF16))
        r = gr_ref[0, sl, :].astype(F32)
        for hh in range(GLA_HEADS):
            vs = slice(hh * GLA_V_DIM, (hh + 1) * GLA_V_DIM)
            o = intra_ref[sl, vs] + inter[hh * CHUNK:(hh + 1) * CHUNK]
            o_ref[0, sl, vs] = (_rms(o, gnw) * _silu(r[:, vs])).astype(BF16)
        st = decay[c] * st + delta_ref[c]
    st_ref[...] = st


def _mlp_kernel(x_ref, oda_ref, ogla_ref, wout_ref, n2_ref, wup_ref, cw_ref, cb_ref, wdown_ref,
                fnw_ref, ut_ref, o_ref, carry_ref, ubuf_ref, act_ref):
    @pl.when(pl.program_id(1) == 0)
    def _():
        carry_ref[...] = ut_ref[...]

    t = x_ref.shape[1]
    mix = jnp.concatenate([oda_ref[0], ogla_ref[0]], axis=1)
    h1 = x_ref[0] + _dot(mix, wout_ref[...])
    o_ref[0] = h1
    xn = _rms(h1, n2_ref[...]).astype(BF16)

    def stage(col0, slot):
        cols = slice(col0, col0 + FF_CHUNK)
        u = _dot(xn, wup_ref[:, cols])
        ubuf_ref[slot, 0:SUB, :] = carry_ref[:, cols]
        ubuf_ref[slot, SUB:SUB + t, :] = u
        carry_ref[:, cols] = u[t - SUB:t]

    def conv(col0, slot):
        cols = slice(col0, col0 + FF_CHUNK)
        w = cw_ref[:, cols]
        return (w[0:1] * ubuf_ref[slot, SUB - 2:SUB - 2 + t, :] + w[1:2] * ubuf_ref[slot, SUB - 1:SUB - 1 + t, :]
                + w[2:3] * ubuf_ref[slot, SUB:SUB + t, :] + cb_ref[:, cols])

    def stage_chunk(j):
        stage(j * FF_CHUNK, (2 * j) % CONV_SLOTS)
        stage(D_FF + j * FF_CHUNK, (2 * j + 1) % CONV_SLOTS)

    n_chunks = D_FF // FF_CHUNK
    stage_chunk(0)
    for j in range(n_chunks):
        if j + 1 < n_chunks:
            stage_chunk(j + 1)
        val = conv(j * FF_CHUNK, (2 * j) % CONV_SLOTS)
        gate = conv(D_FF + j * FF_CHUNK, (2 * j + 1) % CONV_SLOTS)
        act_ref[:, j * FF_CHUNK:(j + 1) * FF_CHUNK] = (val * _silu(gate)).astype(BF16)
    o_ref[0] = _rms(o_ref[0] + _dot(act_ref[...], wdown_ref[...]), fnw_ref[...])


def _bf16_pieces(c):
    out = []
    for _ in range(3):
        p = np.float32(c).astype(BF16).astype(np.float32)
        out.append(p)
        c = np.float32(c) - p
    return out


@functools.lru_cache(maxsize=None)
def _attn_constants():
    assert KEY_TILE == DIAG_TILE == 512 and Q_SUPER <= 1024
    slopes = 2.0 ** (-8.0 * np.arange(1, DA_HEADS + 1, dtype=np.float64) / DA_HEADS)
    c0 = (slopes * LOG2E).astype(np.float32)
    augk = np.zeros((DA_HEADS, KEY_TILE, LANE), np.float32)
    augq = np.zeros((DA_HEADS, LANE, Q_SUPER), np.float32)
    j = np.arange(KEY_TILE)
    iq = np.arange(Q_SUPER)
    for h in range(DA_HEADS):
        a = _bf16_pieces(c0[h])
        for n in range(3):
            augk[h, :, n] = -a[n]
            augq[h, n, :] = iq % 256
            augk[h, :, 3 + n] = -a[n]
            augq[h, 3 + n, :] = (iq // 256) * 256
            augk[h, :, 6 + n] = j % 256
            augq[h, 6 + n, :] = a[n]
            augk[h, :, 9 + n] = (j // 256) * 256
            augq[h, 9 + n, :] = a[n]
    jj, ii = np.meshgrid(np.arange(KEY_TILE), np.arange(DIAG_TILE), indexing="ij")
    visible = (jj // CHUNK) <= (ii // CHUNK)
    bdiag = np.zeros((DA_HEADS, KEY_TILE, 2 * DIAG_TILE), np.float32)
    for h in range(DA_HEADS):
        corr = np.where(jj > ii, -2.0 * c0[h] * (jj - ii), 0.0)
        tile = np.where(visible, corr, -np.inf).astype(np.float32)
        bdiag[h] = np.concatenate([tile, tile], axis=1)
    return c0, augk, augq, bdiag


@functools.lru_cache(maxsize=None)
def _gla_constants():
    r = np.arange(CUMSUM_ROWS)
    return ((r[:, None] // CHUNK == r[None, :] // CHUNK) & (r[None, :] <= r[:, None])).astype(np.float32)


def _full(shape):
    return pl.BlockSpec(shape, lambda *_: (0,) * len(shape))


def _resident(shape):
    return pl.BlockSpec(shape, lambda *_: (0,) * len(shape), pipeline_mode=pl.Buffered(1))


def _params(sem, flags=None):
    return pltpu.CompilerParams(dimension_semantics=sem, vmem_limit_bytes=VMEM_LIMIT, flags=flags)


def kernel(x, meta_tokens, norm1_w, w_in, lambda_q1, lambda_k1, lambda_q2, lambda_k2, da_subln_w,
           gla_gate_w, gla_gate_b, gla_norm_w, w_out, norm2_w, w_up, conv_w, conv_b, w_down,
           final_norm_w):
    bsz, seq, _ = x.shape
    assert seq % Q_SUPER == 0 and seq % TOK_TILE == 0 and seq % GLA_TILE == 0 and seq % MLP_TILE == 0
    assert norm1_w.shape[0] == 1

    w = w_in[0]
    wq, wk, wv = w[:, 0:512], w[:, 512:1024], w[:, 1024:1536]
    wrest, wlr = w[:, 1536:3072], w[:, 3072:3088]
    wnat = jnp.concatenate([wk, wrest, jnp.pad(wlr, ((0, 0), (0, LANE - GLA_GATE_RANK)))], axis=1).astype(BF16)
    wt = jnp.concatenate([wq, wv], axis=1).T.astype(BF16)
    gw = jnp.pad(gla_gate_w[0], ((0, LANE - GLA_GATE_RANK), (0, 0))).astype(BF16)
    gb = gla_gate_b[0][None].astype(F32)
    n1 = norm1_w[0][None]
    n2 = norm2_w[0][None]
    fnw = final_norm_w[None]
    gnw = gla_norm_w[0][None]
    subw_row = da_subln_w[0][None]
    subw_col = jnp.broadcast_to(da_subln_w[0][:, None], (DA_V_DIM, Q_SUPER))
    lam4 = jnp.concatenate([lambda_q1, lambda_k1, lambda_q2, lambda_k2], axis=0)
    wout = w_out[0].astype(BF16)
    wup = w_up[0].astype(BF16)
    wdown = w_down[0].astype(BF16)
    cw = conv_w[0]
    cb = conv_b[0][None]

    c0_np, augk_np, augq_np, bdiag_np = _attn_constants()
    tri_np = _gla_constants()
    c0 = jnp.asarray(c0_np)
    augk = jnp.asarray(augk_np, BF16)
    augq = jnp.asarray(augq_np, BF16)
    bdiag = jnp.asarray(bdiag_np)
    tri = jnp.asarray(tri_np, BF16)

    smem = pl.BlockSpec(memory_space=pltpu.SMEM)
    vmem = pl.BlockSpec(memory_space=pltpu.VMEM)

    k_meta, v_meta, st0, utail = pl.pallas_call(
        _meta_kernel,
        out_shape=(jax.ShapeDtypeStruct((N_META, 512), BF16),
                   jax.ShapeDtypeStruct((N_META, 512), BF16),
                   jax.ShapeDtypeStruct((GLA_V_DIM, GLA_KW), F32),
                   jax.ShapeDtypeStruct((8, 2 * D_FF), F32)),
        in_specs=[vmem, vmem, vmem, vmem, vmem, vmem, vmem, smem, vmem, vmem, vmem, vmem, vmem],
        out_specs=(vmem, vmem, vmem, vmem),
        compiler_params=pltpu.CompilerParams(vmem_limit_bytes=VMEM_LIMIT),
        name="meta_mixer",
    )(meta_tokens, n1, wnat, wt, gw, gb, lam4, c0, subw_row, gnw, wout, n2, wup)
    kmeta = k_meta.reshape(N_META, DA_HEADS, LANE).transpose(1, 0, 2)
    vtmeta = jnp.pad(v_meta.reshape(N_META, DA_HEADS, DA_V_DIM).transpose(1, 2, 0),
                     ((0, 0), (0, 0), (0, LANE - N_META)))

    nt = seq // TOK_TILE
    tok = lambda width: pl.BlockSpec((1, TOK_TILE, width), lambda b, t: (b, t, 0))
    tokt = pl.BlockSpec((1, 512, TOK_TILE), lambda b, t: (b, 0, t))
    kcat, qt, vt, gq, gk, gv, gr, la = pl.pallas_call(
        _proj_kernel,
        grid=(bsz, nt),
        out_shape=(jax.ShapeDtypeStruct((bsz, seq, 512), BF16),
                   jax.ShapeDtypeStruct((bsz, 512, seq), BF16),
                   jax.ShapeDtypeStruct((bsz, 512, seq), BF16),
                   jax.ShapeDtypeStruct((bsz, seq, GLA_KW), BF16),
                   jax.ShapeDtypeStruct((bsz, seq, GLA_KW), BF16),
                   jax.ShapeDtypeStruct((bsz, seq, GLA_WIDTH), BF16),
                   jax.ShapeDtypeStruct((bsz, seq, GLA_WIDTH), BF16),
                   jax.ShapeDtypeStruct((bsz, seq, GLA_KW), F32)),
        in_specs=[tok(D_MODEL), _full((1, D_MODEL)), _resident((D_MODEL, NAT_COLS)),
                  _resident((1024, D_MODEL)), _full((LANE, GLA_KW)), _full((1, GLA_KW))],
        out_specs=(tok(512), tokt, tokt, tok(GLA_KW), tok(GLA_KW), tok(GLA_WIDTH), tok(GLA_WIDTH),
                   tok(GLA_KW)),
        compiler_params=_params(("parallel", "arbitrary")),
        name="in_proj",
    )(x, n1, wnat, wt, gw, gb)

    per_head = lambda shape: pl.BlockSpec((1,) + shape, lambda b, h: (h, 0, 0))
    row = pltpu.VMEM((1, 2 * Q_SUPER), F32)
    o_da = pl.pallas_call(
        _attn_kernel,
        grid=(bsz, DA_HEADS),
        out_shape=jax.ShapeDtypeStruct((bsz, seq, DA_WIDTH), BF16),
        in_specs=[smem, _full((4, DA_QK_DIM)),
                  pl.BlockSpec((1, 2 * DA_QK_DIM, seq), lambda b, h: (b, h, 0)),
                  pl.BlockSpec((1, seq, LANE), lambda b, h: (b, 0, h)),
                  pl.BlockSpec((1, DA_V_DIM, seq), lambda b, h: (b, h, 0)),
                  per_head((KEY_TILE, LANE)), per_head((LANE, Q_SUPER)), per_head((KEY_TILE, 2 * DIAG_TILE)),
                  per_head((N_META, LANE)), per_head((DA_V_DIM, LANE)), _full((DA_V_DIM, Q_SUPER))],
        out_specs=pl.BlockSpec((1, seq, DA_V_DIM), lambda b, h: (b, 0, h)),
        scratch_shapes=[pltpu.VMEM((2 * LANE, 2 * Q_SUPER), BF16),
                        pltpu.VMEM((2 * LANE, 2 * Q_SUPER), BF16),
                        row, row, row, row,
                        pltpu.VMEM((DA_V_DIM, 2 * Q_SUPER), F32),
                        pltpu.VMEM((DA_V_DIM, 2 * Q_SUPER), F32),
                        pltpu.VMEM((KEY_TILE, 2 * Q_SUPER), F32),
                        pltpu.VMEM((KEY_TILE, 2 * Q_SUPER), F32),
                        row, row],
        compiler_params=_params(("parallel", "parallel")),
        name="diff_attn",
    )(c0, lam4, qt, kcat, vt, augk, augq, bdiag, kmeta, vtmeta, subw_col)

    ng = seq // GLA_TILE
    gtok = lambda width: pl.BlockSpec((1, GLA_TILE, width), lambda b, t: (b, t, 0))
    o_gla = pl.pallas_call(
        _gla_kernel,
        grid=(bsz, ng),
        out_shape=jax.ShapeDtypeStruct((bsz, seq, GLA_WIDTH), BF16),
        in_specs=[gtok(GLA_KW), gtok(GLA_KW), gtok(GLA_WIDTH), gtok(GLA_WIDTH), gtok(GLA_KW),
                  _full((CUMSUM_ROWS, CUMSUM_ROWS)), _full((GLA_V_DIM, GLA_KW)), _full((1, GLA_V_DIM))],
        out_specs=gtok(GLA_WIDTH),
        scratch_shapes=[pltpu.VMEM((GLA_V_DIM, GLA_KW), F32),
                        pltpu.VMEM((GLA_TILE // CHUNK, GLA_V_DIM, GLA_KW), F32),
                        pltpu.VMEM((GLA_TILE, GLA_WIDTH), F32),
                        pltpu.VMEM((GLA_TILE // CHUNK, GLA_HEADS * CHUNK, GLA_KW), BF16)],
        compiler_params=_params(("parallel", "arbitrary")),
        name="gla",
    )(gq, gk, gv, gr, la, tri, st0, gnw)

    mtok = lambda width: pl.BlockSpec((1, MLP_TILE, width), lambda b, t: (b, t, 0))
    out = pl.pallas_call(
        _mlp_kernel,
        grid=(bsz, seq // MLP_TILE),
        out_shape=jax.ShapeDtypeStruct((bsz, seq, D_MODEL), F32),
        in_specs=[mtok(D_MODEL), mtok(DA_WIDTH), mtok(GLA_WIDTH), _resident((D_MODEL, D_MODEL)),
                  _full((1, D_MODEL)), _resident((D_MODEL, 2 * D_FF)), _full((3, 2 * D_FF)),
                  _full((1, 2 * D_FF)), _resident((D_FF, D_MODEL)), _full((1, D_MODEL)),
                  _full((8, 2 * D_FF))],
        out_specs=mtok(D_MODEL),
        scratch_shapes=[pltpu.VMEM((8, 2 * D_FF), F32),
                        pltpu.VMEM((CONV_SLOTS, MLP_TILE + 2 * SUB, FF_CHUNK), F32),
                        pltpu.VMEM((MLP_TILE, D_FF), BF16)],
        compiler_params=_params(("parallel", "arbitrary")),
        name="out_proj_mlp",
    )(x, o_da, o_gla, wout, n2, wup, cw, cb, wdown, fnw, utail)
    return out
```
